```python
import jax, jax.numpy as jnp
from jax import lax
import numpy as np

D_MODEL = 1024
BATCH = 8
SEQ = 2048
DEPTH = 4
DEC_BATCH = 2
DEC_SEQ = 8192
PAST_LEN = 128

GRID_W = 64
N_MIXERS = 3
N_LAYERS_A = (DEPTH + 2) // 3
N_LAYERS_B = (DEPTH + 1) // 3
N_LAYERS_C = DEPTH // 3
Q_BLOCK = 128
ROPE_THETA = 10000.0
EPS = 1e-6

MLA_HEADS = 8
MLA_Q_LORA = 384
MLA_KV_LORA = 256
MLA_NOPE = 128
MLA_ROPE = 64
MLA_V = 128

GQA_Q_HEADS = 8
GQA_KV_HEADS = 2
GQA_HEAD_DIM = 128

NA_HEADS = 16
NA_HEAD_DIM = 64
NA_WIN_ROWS_MAX = 8
NA_WIN_COLS = 16
NA_REL_ROWS = 2 * NA_WIN_ROWS_MAX - 1
NA_REL_COLS = 2 * NA_WIN_COLS - 1

D_FF = 4096
CONV_WIDTH = 3
PLE_DIM = 256

kernel_name = "hybrid_mla_gqa_natten_encoder"


def rms_norm(x, g):
    xf = x.astype(jnp.float32)
    y = xf * lax.rsqrt(jnp.mean(xf * xf, axis=-1, keepdims=True) + EPS)
    return (y * g.astype(jnp.float32)).astype(x.dtype)


def rms_norm_plain(x):
    xf = x.astype(jnp.float32)
    return (xf * lax.rsqrt(jnp.mean(xf * xf, axis=-1, keepdims=True) + EPS)).astype(x.dtype)


def rope_tables(pos, dim):
    inv = 1.0 / (ROPE_THETA ** (jnp.arange(0, dim, 2, dtype=jnp.float32) / dim))
    ang = pos.astype(jnp.float32)[:, None] * inv[None, :]
    return jnp.cos(ang), jnp.sin(ang)


def apply_rope(x, cos, sin):
    half = x.shape[-1] // 2
    x1, x2 = x[..., :half], x[..., half:]
    c = cos.astype(x.dtype)
    s = sin.astype(x.dtype)
    return jnp.concatenate([x1 * c - x2 * s, x2 * c + x1 * s], axis=-1)


def to_blocks(x, qb):
    b, s = x.shape[:2]
    return jnp.moveaxis(x.reshape((b, s // qb, qb) + x.shape[2:]), 1, 0)


def from_blocks(y):
    nb, b, qb = y.shape[:3]
    return jnp.moveaxis(y, 0, 1).reshape((b, nb * qb) + y.shape[3:])


def mla_mixer(x, w_down, q_norm, kv_norm, w_uq, w_ukv, w_o):
    b, s, _ = x.shape
    down = x @ w_down
    cq = rms_norm(down[..., :MLA_Q_LORA], q_norm)
    ckv = rms_norm(down[..., MLA_Q_LORA:MLA_Q_LORA + MLA_KV_LORA], kv_norm)
    k_rope = down[..., MLA_Q_LORA + MLA_KV_LORA:]
    q = (cq @ w_uq).reshape(b, s, MLA_HEADS, MLA_NOPE + MLA_ROPE)
    kv = (ckv @ w_ukv).reshape(b, s, MLA_HEADS, MLA_NOPE + MLA_V)
    cos, sin = rope_tables(jnp.arange(s), MLA_ROPE)
    scale = (MLA_NOPE + MLA_ROPE) ** -0.5
    q_nope = q[..., :MLA_NOPE] * scale
    q_rope = apply_rope(q[..., MLA_NOPE:], cos[:, None], sin[:, None]) * scale
    k_nope = kv[..., :MLA_NOPE]
    v = kv[..., MLA_NOPE:]
    k_rope = apply_rope(k_rope, cos, sin)

    def attend(qs):
        qn, qr = qs
        sc = (jnp.einsum('bqhd,bkhd->bhqk', qn, k_nope, preferred_element_type=jnp.float32)
              + jnp.einsum('bqhr,bkr->bhqk', qr, k_rope, preferred_element_type=jnp.float32))
        pr = jax.nn.softmax(sc, axis=-1).astype(v.dtype)
        return jnp.einsum('bhqk,bkhd->bqhd', pr, v)

    o = from_blocks(lax.map(attend, (to_blocks(q_nope, Q_BLOCK), to_blocks(q_rope, Q_BLOCK))))
    return o.reshape(b, s, MLA_HEADS * MLA_V) @ w_o


def gqa_mixer(x, w_qkv, q_norm, k_norm, w_o):
    b, s, _ = x.shape
    hd = GQA_HEAD_DIM
    qkv = x @ w_qkv
    q = qkv[..., :GQA_Q_HEADS * hd].reshape(b, s, GQA_Q_HEADS, hd)
    k = qkv[..., GQA_Q_HEADS * hd:(GQA_Q_HEADS + GQA_KV_HEADS) * hd].reshape(b, s, GQA_KV_HEADS, hd)
    v = qkv[..., (GQA_Q_HEADS + GQA_KV_HEADS) * hd:].reshape(b, s, GQA_KV_HEADS, hd)
    q = rms_norm(q, q_norm)
    k = rms_norm(k, k_norm)
    t = jnp.arange(s)
    half = hd // 2
    rc, rs = rope_tables(t // GRID_W, half)
    cc, cs = rope_tables(t % GRID_W, half)

    def axial(z):
        return jnp.concatenate([apply_rope(z[..., :half], rc[:, None], rs[:, None]),
                                apply_rope(z[..., half:], cc[:, None], cs[:, None])], axis=-1)

    q = axial(q) * hd ** -0.5
    k = axial(k)
    g = GQA_Q_HEADS // GQA_KV_HEADS
    q = q.reshape(b, s, GQA_KV_HEADS, g, hd)

    def attend(qb):
        sc = jnp.einsum('bqkgd,bskd->bkgqs', qb, k, preferred_element_type=jnp.float32)
        pr = jax.nn.softmax(sc, axis=-1).astype(v.dtype)
        return jnp.einsum('bkgqs,bskd->bqkgd', pr, v)

    o = from_blocks(lax.map(attend, to_blocks(q, Q_BLOCK)))
    return o.reshape(b, s, GQA_Q_HEADS * hd) @ w_o


def na_indices(s):
    rows = s // GRID_W
    wr = min(NA_WIN_ROWS_MAX, rows)
    wc = NA_WIN_COLS
    t = np.arange(s)
    r = t // GRID_W
    c = t % GRID_W
    r0 = np.clip(r - wr // 2, 0, rows - wr)
    c0 = np.clip(c - wc // 2, 0, GRID_W - wc)
    kr = r0[:, None] + np.arange(wr)[None, :]
    kc = c0[:, None] + np.arange(wc)[None, :]
    idx = (kr[:, :, None] * GRID_W + kc[:, None, :]).reshape(s, wr * wc)
    rel = (((kr - r[:, None] + NA_WIN_ROWS_MAX - 1)[:, :, None] * NA_REL_COLS)
           + (kc - c[:, None] + NA_WIN_COLS - 1)[:, None, :]).reshape(s, wr * wc)
    return jnp.asarray(idx, dtype=jnp.int32), jnp.asarray(rel, dtype=jnp.int32)


def na_mixer(x, w_qkv, rpb, w_o):
    b, s, _ = x.shape
    qkv = (x @ w_qkv).reshape(b, s, 3, NA_HEADS, NA_HEAD_DIM)
    q = qkv[:, :, 0] * NA_HEAD_DIM ** -0.5
    k = qkv[:, :, 1]
    v = qkv[:, :, 2]
    idx, rel = na_indices(s)
    bias_tab = rpb.reshape(NA_HEADS, NA_REL_ROWS * NA_REL_COLS)
    nb = s // GRID_W

    def attend(blk):
        qb, ib, rb = blk
        kb = k[:, ib]
        vb = v[:, ib]
        sc = (jnp.einsum('bqhd,bqkhd->bhqk', qb, kb, preferred_element_type=jnp.float32)
              + bias_tab[:, rb].astype(jnp.float32)[None])
        pr = jax.nn.softmax(sc, axis=-1).astype(v.dtype)
        return jnp.einsum('bhqk,bqkhd->bqhd', pr, vb)

    o = from_blocks(lax.map(attend, (to_blocks(q, GRID_W),
                                     idx.reshape(nb, GRID_W, -1),
                                     rel.reshape(nb, GRID_W, -1))))
    return o.reshape(b, s, NA_HEADS * NA_HEAD_DIM) @ w_o


def conv_ffn(x, w_in, conv_w, conv_b, w_out):
    gu = x @ w_in
    g, u = gu[..., :D_FF], gu[..., D_FF:]
    gp = jnp.pad(g, ((0, 0), (1, 1), (0, 0)))
    g = gp[:, :-2] * conv_w[0] + gp[:, 1:-1] * conv_w[1] + gp[:, 2:] * conv_w[2] + conv_b
    return (jax.nn.gelu(g, approximate=True) * u) @ w_out


def run_trunk(x, p, w):
    for i in range(DEPTH):
        kind, j = i % N_MIXERS, i // N_MIXERS
        hn = rms_norm(x, w['norm_mix_pre'][i])
        if kind == 0:
            m = mla_mixer(hn, w['mla_w_down'][j], w['mla_q_norm'][j], w['mla_kv_norm'][j],
                          w['mla_w_uq'][j], w['mla_w_ukv'][j], w['mla_w_o'][j])
        elif kind == 1:
            m = gqa_mixer(hn, w['gqa_w_qkv'][j], w['gqa_q_norm'][j], w['gqa_k_norm'][j], w['gqa_w_o'][j])
        else:
            m = na_mixer(hn, w['na_w_qkv'][j], w['na_rpb'][j], w['na_w_o'][j])
        x = x + rms_norm(m, w['norm_mix_post'][i])
        f = conv_ffn(rms_norm(x, w['norm_ffn_pre'][i]), w['ffn_w_in'][i], w['ffn_conv_w'][i],
                     w['ffn_conv_b'][i], w['ffn_w_out'][i])
        x = x + rms_norm(f, w['norm_ffn_post'][i])
        e = p[i] @ w['ple_w_proj'][i]
        gate = jax.nn.sigmoid(rms_norm_plain(x) @ w['ple_w_gate'][i])
        x = x + rms_norm(gate * e, w['ple_norm'][i])
    return x


def setup_inputs(seed: int = 0) -> dict:
    key = jax.random.key(seed)
    ks = jax.random.split(key, 32)
    f32 = jnp.float32

    def lin(k, shape):
        return jax.random.normal(k, shape, f32) * (shape[-2] ** -0.5)

    def gain(k, shape):
        return 1.0 + 0.05 * jax.random.normal(k, shape, f32)

    d = D_MODEL
    return {
        "x_prompt": jax.random.normal(ks[0], (BATCH, SEQ, d), f32),
        "x_sample": jax.random.normal(ks[1], (DEC_BATCH, DEC_SEQ, d), f32),
        "p_prompt": jax.random.normal(ks[2], (DEPTH, BATCH, SEQ, PLE_DIM), f32),
        "p_sample": jax.random.normal(ks[3], (DEPTH, DEC_BATCH, DEC_SEQ, PLE_DIM), f32),
        "norm_mix_pre": gain(ks[4], (DEPTH, d)),
        "norm_mix_post": gain(ks[5], (DEPTH, d)),
        "norm_ffn_pre": gain(ks[6], (DEPTH, d)),
        "norm_ffn_post": gain(ks[7], (DEPTH, d)),
        "mla_w_down": lin(ks[8], (N_LAYERS_A, d, MLA_Q_LORA + MLA_KV_LORA + MLA_ROPE)),
        "mla_q_norm": gain(ks[9], (N_LAYERS_A, MLA_Q_LORA)),
        "mla_kv_norm": gain(ks[10], (N_LAYERS_A, MLA_KV_LORA)),
        "mla_w_uq": lin(ks[11], (N_LAYERS_A, MLA_Q_LORA, MLA_HEADS * (MLA_NOPE + MLA_ROPE))),
        "mla_w_ukv": lin(ks[12], (N_LAYERS_A, MLA_KV_LORA, MLA_HEADS * (MLA_NOPE + MLA_V))),
        "mla_w_o": lin(ks[13], (N_LAYERS_A, MLA_HEADS * MLA_V, d)),
        "gqa_w_qkv": lin(ks[14], (N_LAYERS_B, d, (GQA_Q_HEADS + 2 * GQA_KV_HEADS) * GQA_HEAD_DIM)),
        "gqa_q_norm": gain(ks[15], (N_LAYERS_B, GQA_HEAD_DIM)),
        "gqa_k_norm": gain(ks[16], (N_LAYERS_B, GQA_HEAD_DIM)),
        "gqa_w_o": lin(ks[17], (N_LAYERS_B, GQA_Q_HEADS * GQA_HEAD_DIM, d)),
        "na_w_qkv": lin(ks[18], (N_LAYERS_C, d, 3 * NA_HEADS * NA_HEAD_DIM)),
        "na_rpb": 0.1 * jax.random.normal(ks[19], (N_LAYERS_C, NA_HEADS, NA_REL_ROWS, NA_REL_COLS), f32),
        "na_w_o": lin(ks[20], (N_LAYERS_C, NA_HEADS * NA_HEAD_DIM, d)),
        "ffn_w_in": lin(ks[21], (DEPTH, d, 2 * D_FF)),
        "ffn_conv_w": jax.random.normal(ks[22], (DEPTH, CONV_WIDTH, D_FF), f32) * (CONV_WIDTH ** -0.5),
        "ffn_conv_b": 0.02 * jax.random.normal(ks[23], (DEPTH, D_FF), f32),
        "ffn_w_out": lin(ks[24], (DEPTH, D_FF, d)),
        "ple_w_proj": lin(ks[25], (DEPTH, PLE_DIM, d)),
        "ple_w_gate": lin(ks[26], (DEPTH, d, d)),
        "ple_norm": gain(ks[27], (DEPTH, d)),
    }


def reference(x_prompt, x_sample, p_prompt, p_sample, norm_mix_pre, norm_mix_post, norm_ffn_pre,
              norm_ffn_post, mla_w_down, mla_q_norm, mla_kv_norm, mla_w_uq, mla_w_ukv, mla_w_o,
              gqa_w_qkv, gqa_q_norm, gqa_k_norm, gqa_w_o, na_w_qkv, na_rpb, na_w_o,
              ffn_w_in, ffn_conv_w, ffn_conv_b, ffn_w_out, ple_w_proj, ple_w_gate, ple_norm):
    w = dict(norm_mix_pre=norm_mix_pre, norm_mix_post=norm_mix_post, norm_ffn_pre=norm_ffn_pre,
             norm_ffn_post=norm_ffn_post, mla_w_down=mla_w_down, mla_q_norm=mla_q_norm,
             mla_kv_norm=mla_kv_norm, mla_w_uq=mla_w_uq, mla_w_ukv=mla_w_ukv, mla_w_o=mla_w_o,
             gqa_w_qkv=gqa_w_qkv, gqa_q_norm=gqa_q_norm, gqa_k_norm=gqa_k_norm, gqa_w_o=gqa_w_o,
             na_w_qkv=na_w_qkv, na_rpb=na_rpb, na_w_o=na_w_o, ffn_w_in=ffn_w_in,
             ffn_conv_w=ffn_conv_w, ffn_conv_b=ffn_conv_b, ffn_w_out=ffn_w_out,
             ple_w_proj=ple_w_proj, ple_w_gate=ple_w_gate, ple_norm=ple_norm)
    y_prompt = run_trunk(x_prompt, p_prompt, w)
    y_sample = run_trunk(x_sample, p_sample, w)
    return (y_prompt, y_sample)
```

```python
import functools

import numpy as np
import jax
import jax.numpy as jnp
from jax import lax
from jax.experimental import pallas as pl
from jax.experimental.pallas import tpu as pltpu

F32 = jnp.float32
BF16 = jnp.bfloat16

D_MODEL = 1024
DEPTH = 4
N_MIXERS = 3
GRID_W = 64
ROPE_THETA = 10000.0
EPS = 1e-6

MLA_HEADS = 8
MLA_Q_LORA = 384
MLA_KV_LORA = 256
MLA_NOPE = 128
MLA_ROPE = 64
MLA_V = 128
MLA_QK_PAD = 256

GQA_Q_HEADS = 8
GQA_KV_HEADS = 2
GQA_HEAD_DIM = 128
GQA_GROUP = GQA_Q_HEADS // GQA_KV_HEADS

NA_HEADS = 16
NA_HEAD_DIM = 64
NA_WIN_ROWS = 8
NA_WIN_COLS = 16
NA_REL_ROWS = 2 * NA_WIN_ROWS - 1
NA_REL_COLS = 2 * NA_WIN_COLS - 1
NA_Q_ROWS = 4
NA_TQ = NA_Q_ROWS * GRID_W
NA_TK = 3 * NA_TQ
NA_PAIRS = NA_HEADS * NA_HEAD_DIM // 128

D_FF = 4096
PLE_DIM = 256

NEG = -1e30
V7X_VMEM_LIMIT = 56 * 1024 * 1024
HALO = 16

TM_PROJ = 512
TM_FFN = 512
TF_FFN = 512
TQ_MLA = 512
TQ_GQA = 256
TK_ATTN = 512


def _params(sem):
    return pltpu.CompilerParams(dimension_semantics=sem, vmem_limit_bytes=V7X_VMEM_LIMIT)


def _rms(x):
    return x * lax.rsqrt(jnp.mean(x * x, axis=-1, keepdims=True) + EPS)


def _dot(a, b):
    return jnp.dot(a, b, preferred_element_type=F32)


def _rope_tables(pos, dim):
    inv = 1.0 / (ROPE_THETA ** (jnp.arange(0, dim, 2, dtype=F32) / dim))
    ang = pos.astype(F32)[:, None] * inv[None, :]
    return jnp.cos(ang), jnp.sin(ang)


def _rot_cols(w):
    k, n = w.shape
    w4 = w.reshape(k, n // 64, 2, 32)
    return jnp.concatenate([-w4[:, :, 1:], w4[:, :, :1]], axis=2).reshape(k, n)


def _full(shape):
    return pl.BlockSpec(shape, lambda *_: (0,) * len(shape))


def _mla_pre_kernel(x_ref, g_ref, wd_ref, qn_ref, kvn_ref, wuq_ref, wukv_ref, qt_ref, kt_ref,
                    q_out, k_out, v_out):
    hn = (_rms(x_ref[...]) * g_ref[...]).astype(BF16)
    down = _dot(hn, wd_ref[...])
    cq = (_rms(down[:, :MLA_Q_LORA]) * qn_ref[...]).astype(BF16)
    ckv = (_rms(down[:, MLA_Q_LORA:MLA_Q_LORA + MLA_KV_LORA]) * kvn_ref[...]).astype(BF16)
    kt = kt_ref[...]
    kr = (down[:, 640:768] * kt[:, :128] + down[:, 768:896] * kt[:, 128:]).astype(BF16)
    q = _dot(cq, wuq_ref[...])
    kv = _dot(ckv, wukv_ref[...])
    qt = qt_ref[...]
    for h in range(MLA_HEADS):
        lo = h * MLA_QK_PAD
        q_out[:, lo:lo + MLA_QK_PAD] = (q[:, lo:lo + MLA_QK_PAD] * qt).astype(BF16)
        k_out[:, lo:lo + MLA_NOPE] = kv[:, lo:lo + MLA_NOPE].astype(BF16)
        k_out[:, lo + MLA_NOPE:lo + MLA_QK_PAD] = kr
        v_out[:, h * MLA_V:(h + 1) * MLA_V] = kv[:, lo + MLA_NOPE:lo + MLA_QK_PAD].astype(BF16)


def _mla_pre(x, g, w_down, q_norm, kv_norm, w_uq, w_ukv, seq):
    t = x.shape[0]
    tm = TM_PROJ
    nseq = seq // tm
    wkr = w_down[:, MLA_Q_LORA + MLA_KV_LORA:]
    wkr_rot = _rot_cols(wkr)
    wd = jnp.concatenate([w_down[:, :MLA_Q_LORA + MLA_KV_LORA], wkr, wkr_rot, wkr_rot, wkr], axis=1).astype(BF16)
    wq3 = w_uq.reshape(MLA_Q_LORA, MLA_HEADS, MLA_NOPE + MLA_ROPE)
    wq_rope = wq3[:, :, MLA_NOPE:]
    wq_rot = _rot_cols(wq_rope.reshape(MLA_Q_LORA, -1)).reshape(MLA_Q_LORA, MLA_HEADS, MLA_ROPE)
    wuq = jnp.concatenate([wq3, wq_rot], axis=2).reshape(MLA_Q_LORA, MLA_HEADS * MLA_QK_PAD).astype(BF16)
    wukv = w_ukv.astype(BF16)
    cos, sin = _rope_tables(jnp.arange(seq), MLA_ROPE)
    scale = (MLA_NOPE + MLA_ROPE) ** -0.5
    qt = jnp.concatenate([jnp.full((seq, MLA_NOPE), scale, F32), cos * scale, cos * scale,
                          sin * scale, sin * scale], axis=1)
    kt = jnp.concatenate([cos, cos, sin, sin, sin, sin, cos, cos], axis=1)
    row = lambda i: (i, 0)
    pos = lambda i: (i % nseq, 0)
    return pl.pallas_call(
        _mla_pre_kernel,
        grid=(t // tm,),
        in_specs=[pl.BlockSpec((tm, D_MODEL), row), _full((1, D_MODEL)), _full(wd.shape),
                  _full((1, MLA_Q_LORA)), _full((1, MLA_KV_LORA)), _full(wuq.shape), _full(wukv.shape),
                  pl.BlockSpec((tm, 256), pos), pl.BlockSpec((tm, 256), pos)],
        out_specs=[pl.BlockSpec((tm, MLA_HEADS * MLA_QK_PAD), row),
                   pl.BlockSpec((tm, MLA_HEADS * MLA_QK_PAD), row),
                   pl.BlockSpec((tm, MLA_HEADS * MLA_V), row)],
        out_shape=[jax.ShapeDtypeStruct((t, MLA_HEADS * MLA_QK_PAD), BF16),
                   jax.ShapeDtypeStruct((t, MLA_HEADS * MLA_QK_PAD), BF16),
                   jax.ShapeDtypeStruct((t, MLA_HEADS * MLA_V), BF16)],
        compiler_params=_params(("parallel",)),
    )(x, g.reshape(1, -1), wd, q_norm.reshape(1, -1), kv_norm.reshape(1, -1), wuq, wukv, qt, kt)


def _gqa_pre_kernel(x_ref, g_ref, w_ref, qg_ref, kg_ref, tab_ref, q_out, k_out, v_out):
    hn = (_rms(x_ref[...]) * g_ref[...]).astype(BF16)
    y = _dot(hn, w_ref[...])
    tc = tab_ref[:, :128]
    ts = tab_ref[:, 128:]
    hd = GQA_HEAD_DIM
    nq = GQA_Q_HEADS * hd
    nkv = GQA_KV_HEADS * hd
    rot0 = nq + 2 * nkv
    q_tc = qg_ref[0:1, :] * tc
    q_ts = qg_ref[1:2, :] * ts
    k_tc = kg_ref[0:1, :] * tc
    k_ts = kg_ref[1:2, :] * ts
    for h in range(GQA_Q_HEADS):
        z = y[:, h * hd:(h + 1) * hd]
        zr = y[:, rot0 + h * hd:rot0 + (h + 1) * hd]
        r = lax.rsqrt(jnp.mean(z * z, axis=-1, keepdims=True) + EPS) * (hd ** -0.5)
        q_out[:, h * hd:(h + 1) * hd] = ((z * q_tc + zr * q_ts) * r).astype(BF16)
    for h in range(GQA_KV_HEADS):
        z = y[:, nq + h * hd:nq + (h + 1) * hd]
        zr = y[:, rot0 + nq + h * hd:rot0 + nq + (h + 1) * hd]
        r = lax.rsqrt(jnp.mean(z * z, axis=-1, keepdims=True) + EPS)
        k_out[:, h * hd:(h + 1) * hd] = ((z * k_tc + zr * k_ts) * r).astype(BF16)
    v_out[...] = y[:, nq + nkv:nq + 2 * nkv].astype(BF16)


def _gqa_pre(x, g, w_qkv, q_norm, k_norm, seq):
    t = x.shape[0]
    tm = TM_PROJ
    nseq = seq // tm
    hd = GQA_HEAD_DIM
    nq = GQA_Q_HEADS * hd
    nkv = GQA_KV_HEADS * hd
    w = jnp.concatenate([w_qkv, _rot_cols(w_qkv[:, :nq + nkv])], axis=1).astype(BF16)
    pos = jnp.arange(seq)
    rc, rs = _rope_tables(pos // GRID_W, hd // 2)
    cc, cs = _rope_tables(pos % GRID_W, hd // 2)
    tab = jnp.concatenate([rc, rc, cc, cc, rs, rs, cs, cs], axis=1)

    def gains(gn):
        g4 = gn.reshape(2, 2, 32)
        return jnp.stack([gn, g4[:, ::-1].reshape(hd)])

    row = lambda i: (i, 0)
    return pl.pallas_call(
        _gqa_pre_kernel,
        grid=(t // tm,),
        in_specs=[pl.BlockSpec((tm, D_MODEL), row), _full((1, D_MODEL)), _full(w.shape),
                  _full((2, hd)), _full((2, hd)), pl.BlockSpec((tm, 256), lambda i: (i % nseq, 0))],
        out_specs=[pl.BlockSpec((tm, nq), row), pl.BlockSpec((tm, nkv), row), pl.BlockSpec((tm, nkv), row)],
        out_shape=[jax.ShapeDtypeStruct((t, nq), BF16), jax.ShapeDtypeStruct((t, nkv), BF16),
                   jax.ShapeDtypeStruct((t, nkv), BF16)],
        compiler_params=_params(("parallel",)),
    )(x, g.reshape(1, -1), w, gains(q_norm), gains(k_norm), tab)


def _na_pre_kernel(x_ref, g_ref, w_ref, q_out, k_out, v_out):
    hn = (_rms(x_ref[...]) * g_ref[...]).astype(BF16)
    y = _dot(hn, w_ref[...])
    n = NA_HEADS * NA_HEAD_DIM
    q_out[...] = (y[:, :n] * (NA_HEAD_DIM ** -0.5)).astype(BF16)
    k_out[...] = y[:, n:2 * n].astype(BF16)
    v_out[...] = y[:, 2 * n:].astype(BF16)


def _na_pre(x, g, w_qkv):
    t = x.shape[0]
    tm = TM_PROJ
    n = NA_HEADS * NA_HEAD_DIM
    row = lambda i: (i, 0)
    return pl.pallas_call(
        _na_pre_kernel,
        grid=(t // tm,),
        in_specs=[pl.BlockSpec((tm, D_MODEL), row), _full((1, D_MODEL)), _full((D_MODEL, 3 * n))],
        out_specs=[pl.BlockSpec((tm, n), row)] * 3,
        out_shape=[jax.ShapeDtypeStruct((t, n), BF16)] * 3,
        compiler_params=_params(("parallel",)),
    )(x, g.reshape(1, -1), w_qkv.astype(BF16))


def _attn_kernel(q_ref, k_ref, v_ref, o_ref, *, group, dk, dv, tk):
    tq = q_ref.shape[0]
    if group == 1:
        q = q_ref[...]
    else:
        q = jnp.concatenate([q_ref[:, g * dk:(g + 1) * dk] for g in range(group)], axis=0)
    m_rows = group * tq

    def body(c, carry):
        m, l, acc = carry
        off = pl.multiple_of(c * tk, tk)
        kc = k_ref[pl.ds(off, tk), :]
        vc = v_ref[pl.ds(off, tk), :]
        s = lax.dot_general(q, kc, (((1,), (1,)), ((), ())), preferred_element_type=F32)
        m_new = jnp.maximum(m, jnp.max(s, axis=-1, keepdims=True))
        alpha = jnp.exp(m - m_new)
        p = jnp.exp(s - m_new)
        l = alpha * l + jnp.sum(p, axis=-1, keepdims=True)
        acc = alpha * acc + _dot(p.astype(BF16), vc)
        return m_new, l, acc

    init = (jnp.full((m_rows, 1), NEG, F32), jnp.zeros((m_rows, 1), F32), jnp.zeros((m_rows, dv), F32))
    _, l, acc = lax.fori_loop(0, k_ref.shape[0] // tk, body, init)
    o = (acc * (1.0 / l)).astype(o_ref.dtype)
    for g in range(group):
        o_ref[:, g * dv:(g + 1) * dv] = o[g * tq:(g + 1) * tq]


def _attention(q, k, v, batch, seq, kv_heads, group, dk, dv, tq):
    t = q.shape[0]
    nq = seq // tq
    kern = functools.partial(_attn_kernel, group=group, dk=dk, dv=dv, tk=TK_ATTN)
    return pl.pallas_call(
        kern,
        grid=(batch, kv_heads, nq),
        in_specs=[pl.BlockSpec((tq, group * dk), lambda b, h, i: (b * nq + i, h)),
                  pl.BlockSpec((seq, dk), lambda b, h, i: (b, h)),
                  pl.BlockSpec((seq, dv), lambda b, h, i: (b, h))],
        out_specs=pl.BlockSpec((tq, group * dv), lambda b, h, i: (b * nq + i, h)),
        out_shape=jax.ShapeDtypeStruct((t, kv_heads * group * dv), BF16),
        compiler_params=_params(("parallel", "parallel", "arbitrary")),
    )(q, k, v)


def _na_bias_index():
    rows = 8 * NA_Q_ROWS
    masked = NA_REL_ROWS * NA_REL_COLS
    out = np.empty((3, NA_TQ, NA_TK), np.int32)
    ql = np.arange(NA_TQ)
    kl = np.arange(NA_TK)
    for case, j in enumerate((0, 1, rows // NA_Q_ROWS - 1)):
        r = NA_Q_ROWS * j + ql // GRID_W
        c = ql % GRID_W
        r0 = np.clip(r - NA_WIN_ROWS // 2, 0, rows - NA_WIN_ROWS)
        c0 = np.clip(c - NA_WIN_COLS // 2, 0, GRID_W - NA_WIN_COLS)
        kr = NA_Q_ROWS * (j - 1) + kl // GRID_W
        kc = kl % GRID_W
        ok = ((kr[None, :] >= r0[:, None]) & (kr[None, :] < r0[:, None] + NA_WIN_ROWS)
              & (kc[None, :] >= c0[:, None]) & (kc[None, :] < c0[:, None] + NA_WIN_COLS))
        rel = ((kr[None, :] - r[:, None] + NA_WIN_ROWS - 1) * NA_REL_COLS
               + (kc[None, :] - c[:, None] + NA_WIN_COLS - 1))
        out[case] = np.where(ok, rel, masked)
    return out


def _na_attn_kernel(q_ref, kp_ref, kc_ref, kn_ref, vp_ref, vc_ref, vn_ref, b_ref, o_ref):
    q = q_ref[...]
    k = jnp.concatenate([kp_ref[...], kc_ref[...], kn_ref[...]], axis=0)
    v = jnp.concatenate([vp_ref[...], vc_ref[...], vn_ref[...]], axis=0)
    lane = lax.broadcasted_iota(jnp.int32, (1, 128), 1)
    o = jnp.zeros((NA_TQ, 128), F32)
    for hh in range(2):
        sel = (lane < NA_HEAD_DIM) if hh == 0 else (lane >= NA_HEAD_DIM)
        qh = jnp.where(sel, q, jnp.zeros_like(q))
        vh = jnp.where(sel, v, jnp.zeros_like(v))
        s = lax.dot_general(qh, k, (((1,), (1,)), ((), ())), preferred_element_type=F32) + b_ref[hh]
        m = jnp.max(s, axis=-1, keepdims=True)
        p = jnp.exp(s - m)
        l = jnp.sum(p, axis=-1, keepdims=True)
        o = o + _dot(p.astype(BF16), vh) * (1.0 / l)
    o_ref[...] = o.astype(o_ref.dtype)


def _na_attention(q, k, v, rpb, seq):
    t = q.shape[0]
    nb = t // NA_TQ
    nbs = seq // NA_TQ
    tab = jnp.concatenate([rpb.reshape(NA_HEADS, -1), jnp.full((NA_HEADS, 1), NEG, F32)], axis=1)
    bias = jnp.take(tab, jnp.asarray(_na_bias_index().reshape(-1)), axis=1)
    bias = bias.reshape(NA_HEADS, 3, NA_TQ, NA_TK)

    def prev(p, i):
        return (jnp.where(i % nbs == 0, i, i - 1), p)

    def nxt(p, i):
        return (jnp.where(i % nbs == nbs - 1, i, i + 1), p)

    def case(p, i):
        j = i % nbs
        return (p, jnp.where(j == 0, 0, jnp.where(j == nbs - 1, 2, 1)), 0, 0)

    cur = lambda p, i: (i, p)
    blk = lambda f: pl.BlockSpec((NA_TQ, 128), f)
    return pl.pallas_call(
        _na_attn_kernel,
        grid=(NA_PAIRS, nb),
        in_specs=[blk(cur), blk(prev), blk(cur), blk(nxt), blk(prev), blk(cur), blk(nxt),
                  pl.BlockSpec((2, None, NA_TQ, NA_TK), case)],
        out_specs=blk(cur),
        out_shape=jax.ShapeDtypeStruct((t, NA_HEADS * NA_HEAD_DIM), BF16),
        compiler_params=_params(("parallel", "arbitrary")),
    )(q, k, k, k, v, v, v, bias)


def _oproj_kernel(o_ref, w_ref, x_ref, gpost_ref, gpre_ref, x_out, h_out):
    m = _dot(o_ref[...], w_ref[...])
    x1 = x_ref[...] + _rms(m) * gpost_ref[...]
    x_out[...] = x1
    h_out[...] = (_rms(x1) * gpre_ref[...]).astype(BF16)


def _oproj(o, w_o, x, g_post, g_ffn_pre):
    t = x.shape[0]
    tm = TM_PROJ
    row = lambda i: (i, 0)
    return pl.pallas_call(
        _oproj_kernel,
        grid=(t // tm,),
        in_specs=[pl.BlockSpec((tm, D_MODEL), row), _full((D_MODEL, D_MODEL)), pl.BlockSpec((tm, D_MODEL), row),
                  _full((1, D_MODEL)), _full((1, D_MODEL))],
        out_specs=[pl.BlockSpec((tm, D_MODEL), row), pl.BlockSpec((tm, D_MODEL), row)],
        out_shape=[jax.ShapeDtypeStruct((t, D_MODEL), F32), jax.ShapeDtypeStruct((t, D_MODEL), BF16)],
        compiler_params=_params(("parallel",)),
    )(o, w_o.astype(BF16), x, g_post.reshape(1, -1), g_ffn_pre.reshape(1, -1))


def _ffn_kernel(h_ref, hp_ref, hn_ref, wg_ref, wu_ref, cw_ref, cb_ref, wo_ref, x_ref, gpost_ref,
                p_ref, wproj_ref, wgate_ref, gple_ref, out_ref, hbuf, acc, *, tiles_per_seq):
    i = pl.program_id(0)
    f = pl.program_id(1)
    tm = h_ref.shape[0]

    @pl.when(f == 0)
    def _():
        first = (i % tiles_per_seq) == 0
        last = (i % tiles_per_seq) == tiles_per_seq - 1
        hp = hp_ref[...]
        hx = hn_ref[...]
        hbuf[0:HALO, :] = jnp.where(first, jnp.zeros_like(hp), hp)
        hbuf[HALO:HALO + tm, :] = h_ref[...]
        hbuf[HALO + tm:2 * HALO + tm, :] = jnp.where(last, jnp.zeros_like(hx), hx)
        acc[...] = jnp.zeros_like(acc)

    g = _dot(hbuf[...], wg_ref[...])
    u = _dot(hbuf[HALO:HALO + tm, :], wu_ref[...])
    cw = cw_ref[...]
    gc = (g[HALO - 1:HALO - 1 + tm] * cw[0:1] + g[HALO:HALO + tm] * cw[1:2]
          + g[HALO + 1:HALO + 1 + tm] * cw[2:3] + cb_ref[...])
    act = (jax.nn.gelu(gc, approximate=True) * u).astype(BF16)
    acc[...] += _dot(act, wo_ref[...])

    @pl.when(f == pl.num_programs(1) - 1)
    def _():
        x2 = x_ref[...] + _rms(acc[...]) * gpost_ref[...]
        e = _dot(p_ref[...].astype(BF16), wproj_ref[...])
        z = _dot(_rms(x2).astype(BF16), wgate_ref[...])
        gate = 1.0 / (1.0 + jnp.exp(-z))
        out_ref[...] = x2 + _rms(gate * e) * gple_ref[...]


def _ffn_ple(h, x, p, layer, w_in, conv_w, conv_b, w_out, g_post, w_proj, w_gate, g_ple, seq):
    t = x.shape[0]
    tm, tf = TM_FFN, TF_FFN
    nf = D_FF // tf
    hb = tm // HALO
    last_halo = t // HALO - 1
    w_in = w_in.astype(BF16)
    row = lambda i, f: (i, 0)
    kern = functools.partial(_ffn_kernel, tiles_per_seq=seq // tm)
    return pl.pallas_call(
        kern,
        grid=(t // tm, nf),
        in_specs=[pl.BlockSpec((tm, D_MODEL), row),
                  pl.BlockSpec((HALO, D_MODEL), lambda i, f: (jnp.maximum(i * hb - 1, 0), 0)),
                  pl.BlockSpec((HALO, D_MODEL), lambda i, f: (jnp.minimum((i + 1) * hb, last_halo), 0)),
                  pl.BlockSpec((D_MODEL, tf), lambda i, f: (0, f)),
                  pl.BlockSpec((D_MODEL, tf), lambda i, f: (0, f + nf)),
                  pl.BlockSpec((3, tf), lambda i, f: (0, f)),
                  pl.BlockSpec((1, tf), lambda i, f: (0, f)),
                  pl.BlockSpec((tf, D_MODEL), lambda i, f: (f, 0)),
                  pl.BlockSpec((tm, D_MODEL), row),
                  _full((1, D_MODEL)),
                  pl.BlockSpec((None, tm, PLE_DIM), lambda i, f: (layer, i, 0)),
                  _full((PLE_DIM, D_MODEL)), _full((D_MODEL, D_MODEL)), _full((1, D_MODEL))],
        out_specs=pl.BlockSpec((tm, D_MODEL), row),
        out_shape=jax.ShapeDtypeStruct((t, D_MODEL), F32),
        scratch_shapes=[pltpu.VMEM((tm + 2 * HALO, D_MODEL), BF16), pltpu.VMEM((tm, D_MODEL), F32)],
        compiler_params=_params(("parallel", "arbitrary")),
    )(h, h, h, w_in, w_in, conv_w, conv_b.reshape(1, -1), w_out.astype(BF16), x, g_post.reshape(1, -1),
      p, w_proj.astype(BF16), w_gate.astype(BF16), g_ple.reshape(1, -1))


def _trunk(x, p, w, batch, seq):
    x = x.reshape(batch * seq, D_MODEL)
    p = p.reshape(DEPTH, batch * seq, PLE_DIM)
    for i in range(DEPTH):
        kind, j = i % N_MIXERS, i // N_MIXERS
        g_pre = w['norm_mix_pre'][i]
        if kind == 0:
            q, k, v = _mla_pre(x, g_pre, w['mla_w_down'][j], w['mla_q_norm'][j], w['mla_kv_norm'][j],
                               w['mla_w_uq'][j], w['mla_w_ukv'][j], seq)
            o = _attention(q, k, v, batch, seq, MLA_HEADS, 1, MLA_QK_PAD, MLA_V, TQ_MLA)
            w_o = w['mla_w_o'][j]
        elif kind == 1:
            q, k, v = _gqa_pre(x, g_pre, w['gqa_w_qkv'][j], w['gqa_q_norm'][j], w['gqa_k_norm'][j], seq)
            o = _attention(q, k, v, batch, seq, GQA_KV_HEADS, GQA_GROUP, GQA_HEAD_DIM, GQA_HEAD_DIM, TQ_GQA)
            w_o = w['gqa_w_o'][j]
        else:
            q, k, v = _na_pre(x, g_pre, w['na_w_qkv'][j])
            o = _na_attention(q, k, v, w['na_rpb'][j], seq)
            w_o = w['na_w_o'][j]
        x, h = _oproj(o, w_o, x, w['norm_mix_post'][i], w['norm_ffn_pre'][i])
        x = _ffn_ple(h, x, p, i, w['ffn_w_in'][i], w['ffn_conv_w'][i], w['ffn_conv_b'][i], w['ffn_w_out'][i],
                     w['norm_ffn_post'][i], w['ple_w_proj'][i], w['ple_w_gate'][i], w['ple_norm'][i], seq)
    return x.reshape(batch, seq, D_MODEL)


def kernel(x_prompt, x_sample, p_prompt, p_sample, norm_mix_pre, norm_mix_post, norm_ffn_pre, norm_ffn_post, mla_w_down, mla_q_norm, mla_kv_norm, mla_w_uq, mla_w_ukv, mla_w_o, gqa_w_qkv, gqa_q_norm, gqa_k_norm, gqa_w_o, na_w_qkv, na_rpb, na_w_o, ffn_w_in, ffn_conv_w, ffn_conv_b, ffn_w_out, ple_w_proj, ple_w_gate, ple_norm):
    w = dict(norm_mix_pre=norm_mix_pre, norm_mix_post=norm_mix_post, norm_ffn_pre=norm_ffn_pre,
             norm_ffn_post=norm_ffn_post, mla_w_down=mla_w_down, mla_q_norm=mla_q_norm,
             mla_kv_norm=mla_kv_norm, mla_w_uq=mla_w_uq, mla_w_ukv=mla_w_ukv, mla_w_o=mla_w_o,
             gqa_w_qkv=gqa_w_qkv, gqa_q_norm=gqa_q_norm, gqa_k_norm=gqa_k_norm, gqa_w_o=gqa_w_o,
             na_w_qkv=na_w_qkv, na_rpb=na_rpb, na_w_o=na_w_o, ffn_w_in=ffn_w_in,
             ffn_conv_w=ffn_conv_w, ffn_conv_b=ffn_conv_b, ffn_w_out=ffn_w_out,
             ple_w_proj=ple_w_proj, ple_w_gate=ple_w_gate, ple_norm=ple_norm)
    y_prompt = _trunk(x_prompt, p_prompt, w, *x_prompt.shape[:2])
    y_sample = _trunk(x_sample, p_sample, w, *x_sample.shape[:2])
    return (y_prompt, y_sample)
```

```python
import functools

import numpy as np
import jax
import jax.numpy as jnp
from jax import lax
from jax.experimental import pallas as pl
from jax.experimental.pallas import tpu as pltpu

F32 = jnp.float32
BF16 = jnp.bfloat16

D_MODEL = 1024
DEPTH = 4
N_MIXERS = 3
GRID_W = 64
ROPE_THETA = 10000.0
EPS = 1e-6

MLA_HEADS = 8
MLA_Q_LORA = 384
MLA_KV_LORA = 256
MLA_NOPE = 128
MLA_ROPE = 64
MLA_V = 128
MLA_QK_PAD = 256

GQA_Q_HEADS = 8
GQA_KV_HEADS = 2
GQA_HEAD_DIM = 128
GQA_GROUP = GQA_Q_HEADS // GQA_KV_HEADS

NA_HEADS = 16
NA_HEAD_DIM = 64
NA_WIN_ROWS = 8
NA_WIN_COLS = 16
NA_REL_ROWS = 2 * NA_WIN_ROWS - 1
NA_REL_COLS = 2 * NA_WIN_COLS - 1
NA_Q_ROWS = 4
NA_TQ = NA_Q_ROWS * GRID_W
NA_TK = 3 * NA_TQ
NA_PAIRS = NA_HEADS * NA_HEAD_DIM // 128

D_FF = 4096
PLE_DIM = 256

NEG = -1e30
V7X_VMEM_LIMIT = 56 * 1024 * 1024
HALO = 16

TM_PROJ = 512
TM_FFN = 512
TF_FFN = 512
GELU_C1 = 0.7978845608028654
GELU_C2 = GELU_C1 * 0.044715
TQ_ATTN = 512
TK_ATTN = 512
LOG2E = 1.4426950408889634


def _params(sem):
    return pltpu.CompilerParams(dimension_semantics=sem, vmem_limit_bytes=V7X_VMEM_LIMIT)


def _rms(x):
    return x * lax.rsqrt(jnp.mean(x * x, axis=-1, keepdims=True) + EPS)


def _dot(a, b):
    return jnp.dot(a, b, preferred_element_type=F32)


def _dot_nt(a, b):
    return lax.dot_general(a, b, (((1,), (1,)), ((), ())), preferred_element_type=F32)


def _rope_tables(pos, dim):
    inv = 1.0 / (ROPE_THETA ** (jnp.arange(0, dim, 2, dtype=F32) / dim))
    ang = pos.astype(F32)[:, None] * inv[None, :]
    return jnp.cos(ang), jnp.sin(ang)


def _rot_cols(w):
    k, n = w.shape
    w4 = w.reshape(k, n // 64, 2, 32)
    return jnp.concatenate([-w4[:, :, 1:], w4[:, :, :1]], axis=2).reshape(k, n)


def _full(shape):
    return pl.BlockSpec(shape, lambda *_: (0,) * len(shape))


def _mla_pre_kernel(x_ref, g_ref, wd_ref, qn_ref, kvn_ref, wuq_ref, wuk_ref, wuvt_ref, qt_ref, kt_ref,
                    q_out, k_out, vt_out):
    hn = (_rms(x_ref[...]) * g_ref[...]).astype(BF16)
    down = _dot(hn, wd_ref[...])
    cq = (_rms(down[:, :MLA_Q_LORA]) * qn_ref[...]).astype(BF16)
    ckv = (_rms(down[:, MLA_Q_LORA:MLA_Q_LORA + MLA_KV_LORA]) * kvn_ref[...]).astype(BF16)
    kt = kt_ref[...]
    kr = (down[:, 640:768] * kt[:, :128] + down[:, 768:896] * kt[:, 128:]).astype(BF16)
    q = _dot(cq, wuq_ref[...])
    kn = _dot(ckv, wuk_ref[...])
    vt_out[0] = _dot_nt(wuvt_ref[...], ckv).astype(BF16)
    qt = qt_ref[...]
    for h in range(MLA_HEADS):
        lo = h * MLA_QK_PAD
        q_out[:, lo:lo + MLA_QK_PAD] = (q[:, lo:lo + MLA_QK_PAD] * qt).astype(BF16)
        k_out[:, lo:lo + MLA_NOPE] = kn[:, h * MLA_NOPE:(h + 1) * MLA_NOPE].astype(BF16)
        k_out[:, lo + MLA_NOPE:lo + MLA_QK_PAD] = kr


def _mla_pre(x, g, w_down, q_norm, kv_norm, w_uq, w_ukv, seq):
    t = x.shape[0]
    tm = TK_ATTN
    nseq = seq // tm
    wkr = w_down[:, MLA_Q_LORA + MLA_KV_LORA:]
    wkr_rot = _rot_cols(wkr)
    wd = jnp.concatenate([w_down[:, :MLA_Q_LORA + MLA_KV_LORA], wkr, wkr_rot, wkr_rot, wkr], axis=1).astype(BF16)
    wq3 = w_uq.reshape(MLA_Q_LORA, MLA_HEADS, MLA_NOPE + MLA_ROPE)
    wq_rope = wq3[:, :, MLA_NOPE:]
    wq_rot = _rot_cols(wq_rope.reshape(MLA_Q_LORA, -1)).reshape(MLA_Q_LORA, MLA_HEADS, MLA_ROPE)
    wuq = jnp.concatenate([wq3, wq_rot], axis=2).reshape(MLA_Q_LORA, MLA_HEADS * MLA_QK_PAD).astype(BF16)
    wkv3 = w_ukv.reshape(MLA_KV_LORA, MLA_HEADS, MLA_NOPE + MLA_V)
    wuk = wkv3[:, :, :MLA_NOPE].reshape(MLA_KV_LORA, -1).astype(BF16)
    wuvt = wkv3[:, :, MLA_NOPE:].reshape(MLA_KV_LORA, -1).T.astype(BF16)
    cos, sin = _rope_tables(jnp.arange(seq), MLA_ROPE)
    scale = (MLA_NOPE + MLA_ROPE) ** -0.5 * LOG2E
    qt = jnp.concatenate([jnp.full((seq, MLA_NOPE), scale, F32), cos * scale, cos * scale,
                          sin * scale, sin * scale], axis=1)
    kt = jnp.concatenate([cos, cos, sin, sin, sin, sin, cos, cos], axis=1)
    row = lambda i: (i, 0)
    pos = lambda i: (i % nseq, 0)
    return pl.pallas_call(
        _mla_pre_kernel,
        grid=(t // tm,),
        in_specs=[pl.BlockSpec((tm, D_MODEL), row), _full((1, D_MODEL)), _full(wd.shape),
                  _full((1, MLA_Q_LORA)), _full((1, MLA_KV_LORA)), _full(wuq.shape), _full(wuk.shape),
                  _full(wuvt.shape), pl.BlockSpec((tm, 256), pos), pl.BlockSpec((tm, 256), pos)],
        out_specs=[pl.BlockSpec((tm, MLA_HEADS * MLA_QK_PAD), row),
                   pl.BlockSpec((tm, MLA_HEADS * MLA_QK_PAD), row),
                   pl.BlockSpec((1, MLA_HEADS * MLA_V, tm), lambda i: (i, 0, 0))],
        out_shape=[jax.ShapeDtypeStruct((t, MLA_HEADS * MLA_QK_PAD), BF16),
                   jax.ShapeDtypeStruct((t, MLA_HEADS * MLA_QK_PAD), BF16),
                   jax.ShapeDtypeStruct((t // tm, MLA_HEADS * MLA_V, tm), BF16)],
        compiler_params=_params(("parallel",)),
    )(x, g.reshape(1, -1), wd, q_norm.reshape(1, -1), kv_norm.reshape(1, -1), wuq, wuk, wuvt, qt, kt)


def _gqa_pre_kernel(x_ref, g_ref, w_ref, wvt_ref, qg_ref, kg_ref, tab_ref, q_out, k_out, vt_out):
    hn = (_rms(x_ref[...]) * g_ref[...]).astype(BF16)
    y = _dot(hn, w_ref[...])
    vt_out[0] = _dot_nt(wvt_ref[...], hn).astype(BF16)
    tc = tab_ref[:, :128]
    ts = tab_ref[:, 128:]
    hd = GQA_HEAD_DIM
    nq = GQA_Q_HEADS * hd
    rot0 = nq + GQA_KV_HEADS * hd
    q_tc = qg_ref[0:1, :] * tc
    q_ts = qg_ref[1:2, :] * ts
    k_tc = kg_ref[0:1, :] * tc
    k_ts = kg_ref[1:2, :] * ts
    for h in range(GQA_Q_HEADS):
        z = y[:, h * hd:(h + 1) * hd]
        zr = y[:, rot0 + h * hd:rot0 + (h + 1) * hd]
        r = lax.rsqrt(jnp.mean(z * z, axis=-1, keepdims=True) + EPS) * (hd ** -0.5 * LOG2E)
        q_out[:, h * hd:(h + 1) * hd] = ((z * q_tc + zr * q_ts) * r).astype(BF16)
    for h in range(GQA_KV_HEADS):
        z = y[:, nq + h * hd:nq + (h + 1) * hd]
        zr = y[:, rot0 + nq + h * hd:rot0 + nq + (h + 1) * hd]
        r = lax.rsqrt(jnp.mean(z * z, axis=-1, keepdims=True) + EPS)
        k_out[:, h * hd:(h + 1) * hd] = ((z * k_tc + zr * k_ts) * r).astype(BF16)


def _gqa_pre(x, g, w_qkv, q_norm, k_norm, seq):
    t = x.shape[0]
    tm = TK_ATTN
    nseq = seq // tm
    hd = GQA_HEAD_DIM
    nq = GQA_Q_HEADS * hd
    nkv = GQA_KV_HEADS * hd
    wqk = w_qkv[:, :nq + nkv]
    w = jnp.concatenate([wqk, _rot_cols(wqk)], axis=1).astype(BF16)
    wvt = w_qkv[:, nq + nkv:].T.astype(BF16)
    pos = jnp.arange(seq)
    rc, rs = _rope_tables(pos // GRID_W, hd // 2)
    cc, cs = _rope_tables(pos % GRID_W, hd // 2)
    tab = jnp.concatenate([rc, rc, cc, cc, rs, rs, cs, cs], axis=1)

    def gains(gn):
        g4 = gn.reshape(2, 2, 32)
        return jnp.stack([gn, g4[:, ::-1].reshape(hd)])

    row = lambda i: (i, 0)
    return pl.pallas_call(
        _gqa_pre_kernel,
        grid=(t // tm,),
        in_specs=[pl.BlockSpec((tm, D_MODEL), row), _full((1, D_MODEL)), _full(w.shape), _full(wvt.shape),
                  _full((2, hd)), _full((2, hd)), pl.BlockSpec((tm, 256), lambda i: (i % nseq, 0))],
        out_specs=[pl.BlockSpec((tm, nq), row), pl.BlockSpec((tm, nkv), row),
                   pl.BlockSpec((1, nkv, tm), lambda i: (i, 0, 0))],
        out_shape=[jax.ShapeDtypeStruct((t, nq), BF16), jax.ShapeDtypeStruct((t, nkv), BF16),
                   jax.ShapeDtypeStruct((t // tm, nkv, tm), BF16)],
        compiler_params=_params(("parallel",)),
    )(x, g.reshape(1, -1), w, wvt, gains(q_norm), gains(k_norm), tab)


def _na_pre_kernel(x_ref, g_ref, w_ref, q_out, k_out, v_out):
    hn = (_rms(x_ref[...]) * g_ref[...]).astype(BF16)
    y = _dot(hn, w_ref[...])
    n = NA_HEADS * NA_HEAD_DIM
    q_out[...] = (y[:, :n] * (NA_HEAD_DIM ** -0.5 * LOG2E)).astype(BF16)
    k_out[...] = y[:, n:2 * n].astype(BF16)
    v_out[...] = y[:, 2 * n:].astype(BF16)


def _na_pre(x, g, w_qkv):
    t = x.shape[0]
    tm = TM_PROJ
    n = NA_HEADS * NA_HEAD_DIM
    row = lambda i: (i, 0)
    return pl.pallas_call(
        _na_pre_kernel,
        grid=(t // tm,),
        in_specs=[pl.BlockSpec((tm, D_MODEL), row), _full((1, D_MODEL)), _full((D_MODEL, 3 * n))],
        out_specs=[pl.BlockSpec((tm, n), row)] * 3,
        out_shape=[jax.ShapeDtypeStruct((t, n), BF16)] * 3,
        compiler_params=_params(("parallel",)),
    )(x, g.reshape(1, -1), w_qkv.astype(BF16))


def _attn_kernel(q_ref, k_ref, vt_ref, o_ref, s_buf, acc_ref, *, tk):
    q = q_ref[...]
    tq = q.shape[0]
    nk = k_ref.shape[0] // tk

    def scores(c, slot):
        off = pl.multiple_of(c * tk, tk)
        st = _dot_nt(k_ref[pl.ds(off, tk), :], q)
        s_buf[slot] = st
        return jnp.max(st, axis=0, keepdims=True)

    def accumulate(c, slot, cmax, m, l):
        m_new = jnp.maximum(m, cmax)
        alpha = jnp.exp2(m - m_new)
        pt = jnp.exp2(s_buf[slot] - m_new)
        l = alpha * l + jnp.sum(pt, axis=0, keepdims=True)
        acc_ref[...] = alpha * acc_ref[...] + _dot(vt_ref[c], pt.astype(BF16))
        return m_new, l

    def body(i, carry):
        m, l, cmax0 = carry
        c = 2 * i
        cmax1 = scores(c + 1, 1)
        m, l = accumulate(c, 0, cmax0, m, l)
        cmax0 = scores(c + 2, 0)
        m, l = accumulate(c + 1, 1, cmax1, m, l)
        return m, l, cmax0

    acc_ref[...] = jnp.zeros_like(acc_ref)
    init = (jnp.full((1, tq), NEG, F32), jnp.zeros((1, tq), F32), scores(0, 0))
    m, l, cmax0 = lax.fori_loop(0, nk // 2 - 1, body, init)
    cmax1 = scores(nk - 1, 1)
    m, l = accumulate(nk - 2, 0, cmax0, m, l)
    m, l = accumulate(nk - 1, 1, cmax1, m, l)
    o_ref[...] = (acc_ref[...] * (1.0 / l)).T.astype(o_ref.dtype)


def _attention(q, k, vt, batch, seq, heads, group, dk, dv):
    t = q.shape[0]
    tq, tk = TQ_ATTN, TK_ATTN
    nq = seq // tq
    nk = seq // tk
    assert nk % 2 == 0
    kern = functools.partial(_attn_kernel, tk=tk)
    return pl.pallas_call(
        kern,
        grid=(batch, heads, nq),
        in_specs=[pl.BlockSpec((tq, dk), lambda b, h, i: (b * nq + i, h)),
                  pl.BlockSpec((seq, dk), lambda b, h, i: (b, h // group)),
                  pl.BlockSpec((nk, dv, tk), lambda b, h, i: (b, h // group, 0))],
        out_specs=pl.BlockSpec((tq, dv), lambda b, h, i: (b * nq + i, h)),
        out_shape=jax.ShapeDtypeStruct((t, heads * dv), BF16),
        scratch_shapes=[pltpu.VMEM((2, tk, tq), F32), pltpu.VMEM((dv, tq), F32)],
        compiler_params=_params(("parallel", "parallel", "arbitrary")),
    )(q, k, vt)


def _na_bias(rpb):
    qc = np.arange(GRID_W)[:, None]
    kc = np.arange(GRID_W)[None, :]
    c0 = np.clip(qc - NA_WIN_COLS // 2, 0, GRID_W - NA_WIN_COLS)
    col_ok = (kc >= c0) & (kc < c0 + NA_WIN_COLS)
    rel_col = kc - qc + NA_WIN_COLS - 1
    onehot = ((rel_col[..., None] == np.arange(NA_REL_COLS)) & col_ok[..., None]).astype(np.float32)
    tile = jnp.einsum('hab,qkb->haqk', rpb * LOG2E, jnp.asarray(onehot), precision=lax.Precision.HIGHEST)
    tile = jnp.where(jnp.asarray(col_ok), tile, NEG)
    masked = jnp.full((NA_HEADS, GRID_W, GRID_W), NEG, F32)
    rows = 8 * NA_Q_ROWS
    cases = []
    for j in (0, 1, rows // NA_Q_ROWS - 1):
        blocks = []
        for qi in range(NA_Q_ROWS):
            r = NA_Q_ROWS * j + qi
            r0 = min(max(r - NA_WIN_ROWS // 2, 0), rows - NA_WIN_ROWS)
            tiles = []
            for kj in range(3 * NA_Q_ROWS):
                kr = NA_Q_ROWS * (j - 1) + kj
                ok = r0 <= kr < r0 + NA_WIN_ROWS
                tiles.append(tile[:, kr - r + NA_WIN_ROWS - 1] if ok else masked)
            blocks.append(jnp.concatenate(tiles, axis=-1))
        cases.append(jnp.concatenate(blocks, axis=1))
    return jnp.stack(cases, axis=1)


def _na_attn_kernel(q_ref, kp_ref, kc_ref, kn_ref, vp_ref, vc_ref, vn_ref, b_ref, o_ref):
    q = q_ref[...]
    k = jnp.concatenate([kp_ref[...], kc_ref[...], kn_ref[...]], axis=0)
    v = jnp.concatenate([vp_ref[...], vc_ref[...], vn_ref[...]], axis=0)
    lane = lax.broadcasted_iota(jnp.int32, (1, 128), 1)
    o = jnp.zeros((NA_TQ, 128), F32)
    for hh in range(2):
        sel = (lane < NA_HEAD_DIM) if hh == 0 else (lane >= NA_HEAD_DIM)
        qh = jnp.where(sel, q, jnp.zeros_like(q))
        vh = jnp.where(sel, v, jnp.zeros_like(v))
        s = _dot_nt(qh, k) + b_ref[hh]
        m = jnp.max(s, axis=-1, keepdims=True)
        p = jnp.exp2(s - m)
        l = jnp.sum(p, axis=-1, keepdims=True)
        o = o + _dot(p.astype(BF16), vh) * (1.0 / l)
    o_ref[...] = o.astype(o_ref.dtype)


def _na_attention(q, k, v, rpb, seq):
    t = q.shape[0]
    nb = t // NA_TQ
    nbs = seq // NA_TQ
    bias = _na_bias(rpb)

    def prev(p, i):
        return (jnp.where(i % nbs == 0, i, i - 1), p)

    def nxt(p, i):
        return (jnp.where(i % nbs == nbs - 1, i, i + 1), p)

    def case(p, i):
        j = i % nbs
        return (p, jnp.where(j == 0, 0, jnp.where(j == nbs - 1, 2, 1)), 0, 0)

    cur = lambda p, i: (i, p)
    blk = lambda f: pl.BlockSpec((NA_TQ, 128), f)
    return pl.pallas_call(
        _na_attn_kernel,
        grid=(NA_PAIRS, nb),
        in_specs=[blk(cur), blk(prev), blk(cur), blk(nxt), blk(prev), blk(cur), blk(nxt),
                  pl.BlockSpec((2, None, NA_TQ, NA_TK), case)],
        out_specs=blk(cur),
        out_shape=jax.ShapeDtypeStruct((t, NA_HEADS * NA_HEAD_DIM), BF16),
        compiler_params=_params(("parallel", "arbitrary")),
    )(q, k, k, k, v, v, v, bias)


def _oproj_kernel(o_ref, w_ref, x_ref, gpost_ref, gpre_ref, x_out, h_out):
    m = _dot(o_ref[...], w_ref[...])
    x1 = x_ref[...] + _rms(m) * gpost_ref[...]
    x_out[...] = x1
    h_out[...] = (_rms(x1) * gpre_ref[...]).astype(BF16)


def _oproj(o, w_o, x, g_post, g_ffn_pre):
    t = x.shape[0]
    tm = TM_PROJ
    row = lambda i: (i, 0)
    return pl.pallas_call(
        _oproj_kernel,
        grid=(t // tm,),
        in_specs=[pl.BlockSpec((tm, D_MODEL), row), _full((D_MODEL, D_MODEL)), pl.BlockSpec((tm, D_MODEL), row),
                  _full((1, D_MODEL)), _full((1, D_MODEL))],
        out_specs=[pl.BlockSpec((tm, D_MODEL), row), pl.BlockSpec((tm, D_MODEL), row)],
        out_shape=[jax.ShapeDtypeStruct((t, D_MODEL), F32), jax.ShapeDtypeStruct((t, D_MODEL), BF16)],
        compiler_params=_params(("parallel",)),
    )(o, w_o.astype(BF16), x, g_post.reshape(1, -1), g_ffn_pre.reshape(1, -1))


def _ffn_kernel(h_ref, hp_ref, hn_ref, wg_ref, wu_ref, cw_ref, cb_ref, wo_ref, x_ref, gpost_ref,
                p_ref, wproj_ref, wgate_ref, gple_ref, out_ref, hbuf, acc, *, tiles_per_seq):
    i = pl.program_id(0)
    f = pl.program_id(1)
    tm = h_ref.shape[0]

    @pl.when(f == 0)
    def _():
        first = (i % tiles_per_seq) == 0
        last = (i % tiles_per_seq) == tiles_per_seq - 1
        hp = hp_ref[...]
        hx = hn_ref[...]
        hbuf[0:HALO, :] = jnp.where(first, jnp.zeros_like(hp), hp)
        hbuf[HALO:HALO + tm, :] = h_ref[...]
        hbuf[HALO + tm:2 * HALO + tm, :] = jnp.where(last, jnp.zeros_like(hx), hx)
        acc[...] = jnp.zeros_like(acc)

    g = _dot(hbuf[...], wg_ref[...])
    u = _dot(hbuf[HALO:HALO + tm, :], wu_ref[...])
    gc = (g[HALO - 1:HALO - 1 + tm] * cw_ref[0:1, :] + g[HALO:HALO + tm] * cw_ref[1:2, :]
          + g[HALO + 1:HALO + 1 + tm] * cw_ref[2:3, :] + cb_ref[...])
    t = jnp.tanh(gc * (GELU_C1 + GELU_C2 * (gc * gc)))
    act = ((gc * u) * (0.5 + 0.5 * t)).astype(BF16)
    acc[...] += _dot(act, wo_ref[...])

    @pl.when(f == pl.num_programs(1) - 1)
    def _():
        x2 = x_ref[...] + _rms(acc[...]) * gpost_ref[...]
        e = _dot(p_ref[...].astype(BF16), wproj_ref[...])
        z = _dot(_rms(x2).astype(BF16), wgate_ref[...])
        gate = 1.0 / (1.0 + jnp.exp(-z))
        out_ref[...] = x2 + _rms(gate * e) * gple_ref[...]


def _ffn_ple(h, x, p, layer, w_in, conv_w, conv_b, w_out, g_post, w_proj, w_gate, g_ple, seq):
    t = x.shape[0]
    tm, tf = TM_FFN, TF_FFN
    nf = D_FF // tf
    hb = tm // HALO
    last_halo = t // HALO - 1
    w_in = w_in.astype(BF16)
    row = lambda i, f: (i, 0)
    kern = functools.partial(_ffn_kernel, tiles_per_seq=seq // tm)
    return pl.pallas_call(
        kern,
        grid=(t // tm, nf),
        in_specs=[pl.BlockSpec((tm, D_MODEL), row),
                  pl.BlockSpec((HALO, D_MODEL), lambda i, f: (jnp.maximum(i * hb - 1, 0), 0)),
                  pl.BlockSpec((HALO, D_MODEL), lambda i, f: (jnp.minimum((i + 1) * hb, last_halo), 0)),
                  pl.BlockSpec((D_MODEL, tf), lambda i, f: (0, f)),
                  pl.BlockSpec((D_MODEL, tf), lambda i, f: (0, f + nf)),
                  pl.BlockSpec((3, tf), lambda i, f: (0, f)),
                  pl.BlockSpec((1, tf), lambda i, f: (0, f)),
                  pl.BlockSpec((tf, D_MODEL), lambda i, f: (f, 0)),
                  pl.BlockSpec((tm, D_MODEL), row),
                  _full((1, D_MODEL)),
                  pl.BlockSpec((None, tm, PLE_DIM), lambda i, f: (layer, i, 0)),
                  _full((PLE_DIM, D_MODEL)), _full((D_MODEL, D_MODEL)), _full((1, D_MODEL))],
        out_specs=pl.BlockSpec((tm, D_MODEL), row),
        out_shape=jax.ShapeDtypeStruct((t, D_MODEL), F32),
        scratch_shapes=[pltpu.VMEM((tm + 2 * HALO, D_MODEL), BF16), pltpu.VMEM((tm, D_MODEL), F32)],
        compiler_params=_params(("parallel", "arbitrary")),
    )(h, h, h, w_in, w_in, conv_w, conv_b.reshape(1, -1), w_out.astype(BF16), x, g_post.reshape(1, -1),
      p, w_proj.astype(BF16), w_gate.astype(BF16), g_ple.reshape(1, -1))


def _trunk(x, p, w, batch, seq):
    x = x.reshape(batch * seq, D_MODEL)
    p = p.reshape(DEPTH, batch * seq, PLE_DIM)
    for i in range(DEPTH):
        kind, j = i % N_MIXERS, i // N_MIXERS
        g_pre = w['norm_mix_pre'][i]
        if kind == 0:
            q, k, vt = _mla_pre(x, g_pre, w['mla_w_down'][j], w['mla_q_norm'][j], w['mla_kv_norm'][j],
                                w['mla_w_uq'][j], w['mla_w_ukv'][j], seq)
            o = _attention(q, k, vt, batch, seq, MLA_HEADS, 1, MLA_QK_PAD, MLA_V)
            w_o = w['mla_w_o'][j]
        elif kind == 1:
            q, k, vt = _gqa_pre(x, g_pre, w['gqa_w_qkv'][j], w['gqa_q_norm'][j], w['gqa_k_norm'][j], seq)
            o = _attention(q, k, vt, batch, seq, GQA_Q_HEADS, GQA_GROUP, GQA_HEAD_DIM, GQA_HEAD_DIM)
            w_o = w['gqa_w_o'][j]
        else:
            q, k, v = _na_pre(x, g_pre, w['na_w_qkv'][j])
            o = _na_attention(q, k, v, w['na_rpb'][j], seq)
            w_o = w['na_w_o'][j]
        x, h = _oproj(o, w_o, x, w['norm_mix_post'][i], w['norm_ffn_pre'][i])
        x = _ffn_ple(h, x, p, i, w['ffn_w_in'][i], w['ffn_conv_w'][i], w['ffn_conv_b'][i], w['ffn_w_out'][i],
                     w['norm_ffn_post'][i], w['ple_w_proj'][i], w['ple_w_gate'][i], w['ple_norm'][i], seq)
    return x.reshape(batch, seq, D_MODEL)


def kernel(x_prompt, x_sample, p_prompt, p_sample, norm_mix_pre, norm_mix_post, norm_ffn_pre, norm_ffn_post, mla_w_down, mla_q_norm, mla_kv_norm, mla_w_uq, mla_w_ukv, mla_w_o, gqa_w_qkv, gqa_q_norm, gqa_k_norm, gqa_w_o, na_w_qkv, na_rpb, na_w_o, ffn_w_in, ffn_conv_w, ffn_conv_b, ffn_w_out, ple_w_proj, ple_w_gate, ple_norm):
    w = dict(norm_mix_pre=norm_mix_pre, norm_mix_post=norm_mix_post, norm_ffn_pre=norm_ffn_pre,
             norm_ffn_post=norm_ffn_post, mla_w_down=mla_w_down, mla_q_norm=mla_q_norm,
             mla_kv_norm=mla_kv_norm, mla_w_uq=mla_w_uq, mla_w_ukv=mla_w_ukv, mla_w_o=mla_w_o,
             gqa_w_qkv=gqa_w_qkv, gqa_q_norm=gqa_q_norm, gqa_k_norm=gqa_k_norm, gqa_w_o=gqa_w_o,
             na_w_qkv=na_w_qkv, na_rpb=na_rpb, na_w_o=na_w_o, ffn_w_in=ffn_w_in,
             ffn_conv_w=ffn_conv_w, ffn_conv_b=ffn_conv_b, ffn_w_out=ffn_w_out,
             ple_w_proj=ple_w_proj, ple_w_gate=ple_w_gate, ple_norm=ple_norm)
    y_prompt = _trunk(x_prompt, p_prompt, w, *x_prompt.shape[:2])
    y_sample = _trunk(x_sample, p_sample, w, *x_sample.shape[:2])
    return (y_prompt, y_sample)
```

```python
import functools

import numpy as np
import jax
import jax.numpy as jnp
from jax import lax
from jax.experimental import pallas as pl
from jax.experimental.pallas import tpu as pltpu

F32 = jnp.float32
BF16 = jnp.bfloat16

D_MODEL = 1024
DEPTH = 4
N_MIXERS = 3
GRID_W = 64
ROPE_THETA = 10000.0
EPS = 1e-6

MLA_HEADS = 8
MLA_Q_LORA = 384
MLA_KV_LORA = 256
MLA_NOPE = 128
MLA_ROPE = 64
MLA_V = 128
MLA_QK_PAD = 256

GQA_Q_HEADS = 8
GQA_KV_HEADS = 2
GQA_HEAD_DIM = 128
GQA_GROUP = GQA_Q_HEADS // GQA_KV_HEADS

NA_HEADS = 16
NA_HEAD_DIM = 64
NA_WIN_ROWS = 8
NA_WIN_COLS = 16
NA_REL_ROWS = 2 * NA_WIN_ROWS - 1
NA_REL_COLS = 2 * NA_WIN_COLS - 1
NA_Q_ROWS = 4
NA_TQ = NA_Q_ROWS * GRID_W
NA_TK = 3 * NA_TQ
NA_PAIRS = NA_HEADS * NA_HEAD_DIM // 128

D_FF = 4096
PLE_DIM = 256

NEG = -1e30
V7X_VMEM_LIMIT = 56 * 1024 * 1024
HALO = 16

TM_PROJ = 512
TM_FFN = 512
TF_FFN = 1024
PERM_STRIDE = TM_FFN // 8
GELU_C1 = 0.7978845608028654
GELU_C2 = GELU_C1 * 0.044715
TQ_ATTN = 512
TK_ATTN = 512
TKS_ATTN = 1024
LOG2E = 1.4426950408889634


def _params(sem):
    return pltpu.CompilerParams(dimension_semantics=sem, vmem_limit_bytes=V7X_VMEM_LIMIT)


def _rms(x):
    return x * lax.rsqrt(jnp.mean(x * x, axis=-1, keepdims=True) + EPS)


def _dot(a, b):
    return jnp.dot(a, b, preferred_element_type=F32)


def _dot_nt(a, b):
    return lax.dot_general(a, b, (((1,), (1,)), ((), ())), preferred_element_type=F32)


def _rope_tables(pos, dim):
    inv = 1.0 / (ROPE_THETA ** (jnp.arange(0, dim, 2, dtype=F32) / dim))
    ang = pos.astype(F32)[:, None] * inv[None, :]
    return jnp.cos(ang), jnp.sin(ang)


def _rot_cols(w):
    k, n = w.shape
    w4 = w.reshape(k, n // 64, 2, 32)
    return jnp.concatenate([-w4[:, :, 1:], w4[:, :, :1]], axis=2).reshape(k, n)


def _full(shape):
    return pl.BlockSpec(shape, lambda *_: (0,) * len(shape))


def _mla_pre_kernel(x_ref, g_ref, wd_ref, qn_ref, kvn_ref, wuq_ref, wuk_ref, wuvt_ref, qt_ref, kt_ref,
                    q_out, k_out, vt_out):
    hn = (_rms(x_ref[...]) * g_ref[...]).astype(BF16)
    down = _dot(hn, wd_ref[...])
    cq = (_rms(down[:, :MLA_Q_LORA]) * qn_ref[...]).astype(BF16)
    ckv = (_rms(down[:, MLA_Q_LORA:MLA_Q_LORA + MLA_KV_LORA]) * kvn_ref[...]).astype(BF16)
    kt = kt_ref[...]
    kr = (down[:, 640:768] * kt[:, :128] + down[:, 768:896] * kt[:, 128:]).astype(BF16)
    q = _dot(cq, wuq_ref[...])
    kn = _dot(ckv, wuk_ref[...])
    vt_out[0] = _dot_nt(wuvt_ref[...], ckv).astype(BF16)
    qt = qt_ref[...]
    for h in range(MLA_HEADS):
        lo = h * MLA_QK_PAD
        q_out[:, lo:lo + MLA_QK_PAD] = (q[:, lo:lo + MLA_QK_PAD] * qt).astype(BF16)
        k_out[:, lo:lo + MLA_NOPE] = kn[:, h * MLA_NOPE:(h + 1) * MLA_NOPE].astype(BF16)
        k_out[:, lo + MLA_NOPE:lo + MLA_QK_PAD] = kr


def _mla_pre(x, g, w_down, q_norm, kv_norm, w_uq, w_ukv, seq):
    t = x.shape[0]
    tm = TK_ATTN
    nseq = seq // tm
    wkr = w_down[:, MLA_Q_LORA + MLA_KV_LORA:]
    wkr_rot = _rot_cols(wkr)
    wd = jnp.concatenate([w_down[:, :MLA_Q_LORA + MLA_KV_LORA], wkr, wkr_rot, wkr_rot, wkr], axis=1).astype(BF16)
    wq3 = w_uq.reshape(MLA_Q_LORA, MLA_HEADS, MLA_NOPE + MLA_ROPE)
    wq_rope = wq3[:, :, MLA_NOPE:]
    wq_rot = _rot_cols(wq_rope.reshape(MLA_Q_LORA, -1)).reshape(MLA_Q_LORA, MLA_HEADS, MLA_ROPE)
    wuq = jnp.concatenate([wq3, wq_rot], axis=2).reshape(MLA_Q_LORA, MLA_HEADS * MLA_QK_PAD).astype(BF16)
    wkv3 = w_ukv.reshape(MLA_KV_LORA, MLA_HEADS, MLA_NOPE + MLA_V)
    wuk = wkv3[:, :, :MLA_NOPE].reshape(MLA_KV_LORA, -1).astype(BF16)
    wuvt = wkv3[:, :, MLA_NOPE:].reshape(MLA_KV_LORA, -1).T.astype(BF16)
    cos, sin = _rope_tables(jnp.arange(seq), MLA_ROPE)
    scale = (MLA_NOPE + MLA_ROPE) ** -0.5 * LOG2E
    qt = jnp.concatenate([jnp.full((seq, MLA_NOPE), scale, F32), cos * scale, cos * scale,
                          sin * scale, sin * scale], axis=1)
    kt = jnp.concatenate([cos, cos, sin, sin, sin, sin, cos, cos], axis=1)
    row = lambda i: (i, 0)
    pos = lambda i: (i % nseq, 0)
    return pl.pallas_call(
        _mla_pre_kernel,
        grid=(t // tm,),
        in_specs=[pl.BlockSpec((tm, D_MODEL), row), _full((1, D_MODEL)), _full(wd.shape),
                  _full((1, MLA_Q_LORA)), _full((1, MLA_KV_LORA)), _full(wuq.shape), _full(wuk.shape),
                  _full(wuvt.shape), pl.BlockSpec((tm, 256), pos), pl.BlockSpec((tm, 256), pos)],
        out_specs=[pl.BlockSpec((tm, MLA_HEADS * MLA_QK_PAD), row),
                   pl.BlockSpec((tm, MLA_HEADS * MLA_QK_PAD), row),
                   pl.BlockSpec((1, MLA_HEADS * MLA_V, tm), lambda i: (i, 0, 0))],
        out_shape=[jax.ShapeDtypeStruct((t, MLA_HEADS * MLA_QK_PAD), BF16),
                   jax.ShapeDtypeStruct((t, MLA_HEADS * MLA_QK_PAD), BF16),
                   jax.ShapeDtypeStruct((t // tm, MLA_HEADS * MLA_V, tm), BF16)],
        compiler_params=_params(("parallel",)),
    )(x, g.reshape(1, -1), wd, q_norm.reshape(1, -1), kv_norm.reshape(1, -1), wuq, wuk, wuvt, qt, kt)


def _gqa_pre_kernel(x_ref, g_ref, w_ref, wvt_ref, qg_ref, kg_ref, tab_ref, q_out, k_out, vt_out):
    hn = (_rms(x_ref[...]) * g_ref[...]).astype(BF16)
    y = _dot(hn, w_ref[...])
    vt_out[0] = _dot_nt(wvt_ref[...], hn).astype(BF16)
    tc = tab_ref[:, :128]
    ts = tab_ref[:, 128:]
    hd = GQA_HEAD_DIM
    nq = GQA_Q_HEADS * hd
    rot0 = nq + GQA_KV_HEADS * hd
    q_tc = qg_ref[0:1, :] * tc
    q_ts = qg_ref[1:2, :] * ts
    k_tc = kg_ref[0:1, :] * tc
    k_ts = kg_ref[1:2, :] * ts
    for h in range(GQA_Q_HEADS):
        z = y[:, h * hd:(h + 1) * hd]
        zr = y[:, rot0 + h * hd:rot0 + (h + 1) * hd]
        r = lax.rsqrt(jnp.mean(z * z, axis=-1, keepdims=True) + EPS) * (hd ** -0.5 * LOG2E)
        q_out[:, h * hd:(h + 1) * hd] = ((z * q_tc + zr * q_ts) * r).astype(BF16)
    for h in range(GQA_KV_HEADS):
        z = y[:, nq + h * hd:nq + (h + 1) * hd]
        zr = y[:, rot0 + nq + h * hd:rot0 + nq + (h + 1) * hd]
        r = lax.rsqrt(jnp.mean(z * z, axis=-1, keepdims=True) + EPS)
        k_out[:, h * hd:(h + 1) * hd] = ((z * k_tc + zr * k_ts) * r).astype(BF16)


def _gqa_pre(x, g, w_qkv, q_norm, k_norm, seq):
    t = x.shape[0]
    tm = TK_ATTN
    nseq = seq // tm
    hd = GQA_HEAD_DIM
    nq = GQA_Q_HEADS * hd
    nkv = GQA_KV_HEADS * hd
    wqk = w_qkv[:, :nq + nkv]
    w = jnp.concatenate([wqk, _rot_cols(wqk)], axis=1).astype(BF16)
    wvt = w_qkv[:, nq + nkv:].T.astype(BF16)
    pos = jnp.arange(seq)
    rc, rs = _rope_tables(pos // GRID_W, hd // 2)
    cc, cs = _rope_tables(pos % GRID_W, hd // 2)
    tab = jnp.concatenate([rc, rc, cc, cc, rs, rs, cs, cs], axis=1)

    def gains(gn):
        g4 = gn.reshape(2, 2, 32)
        return jnp.stack([gn, g4[:, ::-1].reshape(hd)])

    row = lambda i: (i, 0)
    return pl.pallas_call(
        _gqa_pre_kernel,
        grid=(t // tm,),
        in_specs=[pl.BlockSpec((tm, D_MODEL), row), _full((1, D_MODEL)), _full(w.shape), _full(wvt.shape),
                  _full((2, hd)), _full((2, hd)), pl.BlockSpec((tm, 256), lambda i: (i % nseq, 0))],
        out_specs=[pl.BlockSpec((tm, nq), row), pl.BlockSpec((tm, nkv), row),
                   pl.BlockSpec((1, nkv, tm), lambda i: (i, 0, 0))],
        out_shape=[jax.ShapeDtypeStruct((t, nq), BF16), jax.ShapeDtypeStruct((t, nkv), BF16),
                   jax.ShapeDtypeStruct((t // tm, nkv, tm), BF16)],
        compiler_params=_params(("parallel",)),
    )(x, g.reshape(1, -1), w, wvt, gains(q_norm), gains(k_norm), tab)


def _na_pre_kernel(x_ref, g_ref, w_ref, q_out, k_out, v_out):
    hn = (_rms(x_ref[...]) * g_ref[...]).astype(BF16)
    y = _dot(hn, w_ref[...])
    n = NA_HEADS * NA_HEAD_DIM
    q_out[...] = (y[:, :n] * (NA_HEAD_DIM ** -0.5 * LOG2E)).astype(BF16)
    k_out[...] = y[:, n:2 * n].astype(BF16)
    v_out[...] = y[:, 2 * n:].astype(BF16)


def _na_pre(x, g, w_qkv):
    t = x.shape[0]
    tm = TM_PROJ
    n = NA_HEADS * NA_HEAD_DIM
    row = lambda i: (i, 0)
    return pl.pallas_call(
        _na_pre_kernel,
        grid=(t // tm,),
        in_specs=[pl.BlockSpec((tm, D_MODEL), row), _full((1, D_MODEL)), _full((D_MODEL, 3 * n))],
        out_specs=[pl.BlockSpec((tm, n), row)] * 3,
        out_shape=[jax.ShapeDtypeStruct((t, n), BF16)] * 3,
        compiler_params=_params(("parallel",)),
    )(x, g.reshape(1, -1), w_qkv.astype(BF16))


def _attn_kernel(q_ref, k_ref, vt_ref, o_ref, s_buf, acc_ref, *, tks):
    q = q_ref[...]
    tq = q.shape[0]
    tkv = vt_ref.shape[2]
    per = tks // tkv
    nk = k_ref.shape[0] // tks

    def scores(c, slot):
        off = pl.multiple_of(c * tks, tks)
        st = _dot_nt(k_ref[pl.ds(off, tks), :], q)
        s_buf[slot] = st
        return jnp.max(st, axis=0, keepdims=True)

    def accumulate(c, slot, cmax, m, l):
        m_new = jnp.maximum(m, cmax)
        alpha = jnp.exp2(m - m_new)
        pt = jnp.exp2(s_buf[slot] - m_new)
        l = alpha * l + jnp.sum(pt, axis=0, keepdims=True)
        pb = pt.astype(BF16)
        pv = _dot(vt_ref[c * per], pb[0:tkv])
        for j in range(1, per):
            pv = pv + _dot(vt_ref[c * per + j], pb[j * tkv:(j + 1) * tkv])
        acc_ref[...] = alpha * acc_ref[...] + pv
        return m_new, l

    def body(i, carry):
        m, l, cmax0 = carry
        c = 2 * i
        cmax1 = scores(c + 1, 1)
        m, l = accumulate(c, 0, cmax0, m, l)
        cmax0 = scores(c + 2, 0)
        m, l = accumulate(c + 1, 1, cmax1, m, l)
        return m, l, cmax0

    acc_ref[...] = jnp.zeros_like(acc_ref)
    init = (jnp.full((1, tq), NEG, F32), jnp.zeros((1, tq), F32), scores(0, 0))
    m, l, cmax0 = lax.fori_loop(0, nk // 2 - 1, body, init)
    cmax1 = scores(nk - 1, 1)
    m, l = accumulate(nk - 2, 0, cmax0, m, l)
    m, l = accumulate(nk - 1, 1, cmax1, m, l)
    o_ref[...] = (acc_ref[...] * (1.0 / l)).T.astype(o_ref.dtype)


def _attention(q, k, vt, batch, seq, heads, group, dk, dv):
    t = q.shape[0]
    tq, tk, tks = TQ_ATTN, TK_ATTN, TKS_ATTN
    nq = seq // tq
    nk = seq // tk
    assert (seq // tks) % 2 == 0
    kern = functools.partial(_attn_kernel, tks=tks)
    return pl.pallas_call(
        kern,
        grid=(batch, heads, nq),
        in_specs=[pl.BlockSpec((tq, dk), lambda b, h, i: (b * nq + i, h)),
                  pl.BlockSpec((seq, dk), lambda b, h, i: (b, h // group)),
                  pl.BlockSpec((nk, dv, tk), lambda b, h, i: (b, h // group, 0))],
        out_specs=pl.BlockSpec((tq, dv), lambda b, h, i: (b * nq + i, h)),
        out_shape=jax.ShapeDtypeStruct((t, heads * dv), BF16),
        scratch_shapes=[pltpu.VMEM((2, tks, tq), F32), pltpu.VMEM((dv, tq), F32)],
        compiler_params=_params(("parallel", "parallel", "arbitrary")),
    )(q, k, vt)


def _na_bias(rpb):
    qc = np.arange(GRID_W)[:, None]
    kc = np.arange(GRID_W)[None, :]
    c0 = np.clip(qc - NA_WIN_COLS // 2, 0, GRID_W - NA_WIN_COLS)
    col_ok = (kc >= c0) & (kc < c0 + NA_WIN_COLS)
    rel_col = kc - qc + NA_WIN_COLS - 1
    onehot = ((rel_col[..., None] == np.arange(NA_REL_COLS)) & col_ok[..., None]).astype(np.float32)
    tile = jnp.einsum('hab,qkb->haqk', rpb * LOG2E, jnp.asarray(onehot), precision=lax.Precision.HIGHEST)
    tile = jnp.where(jnp.asarray(col_ok), tile, NEG)
    masked = jnp.full((NA_HEADS, GRID_W, GRID_W), NEG, F32)
    rows = 8 * NA_Q_ROWS
    cases = []
    for j in (0, 1, rows // NA_Q_ROWS - 1):
        blocks = []
        for qi in range(NA_Q_ROWS):
            r = NA_Q_ROWS * j + qi
            r0 = min(max(r - NA_WIN_ROWS // 2, 0), rows - NA_WIN_ROWS)
            tiles = []
            for kj in range(3 * NA_Q_ROWS):
                kr = NA_Q_ROWS * (j - 1) + kj
                ok = r0 <= kr < r0 + NA_WIN_ROWS
                tiles.append(tile[:, kr - r + NA_WIN_ROWS - 1] if ok else masked)
            blocks.append(jnp.concatenate(tiles, axis=-1))
        cases.append(jnp.concatenate(blocks, axis=1))
    return jnp.stack(cases, axis=1)


def _na_attn_kernel(q_ref, kp_ref, kc_ref, kn_ref, vp_ref, vc_ref, vn_ref, b_ref, o_ref):
    q = q_ref[...]
    k = jnp.concatenate([kp_ref[...], kc_ref[...], kn_ref[...]], axis=0)
    v = jnp.concatenate([vp_ref[...], vc_ref[...], vn_ref[...]], axis=0)
    lane = lax.broadcasted_iota(jnp.int32, (1, 128), 1)
    o = jnp.zeros((NA_TQ, 128), F32)
    for hh in range(2):
        sel = (lane < NA_HEAD_DIM) if hh == 0 else (lane >= NA_HEAD_DIM)
        qh = jnp.where(sel, q, jnp.zeros_like(q))
        vh = jnp.where(sel, v, jnp.zeros_like(v))
        s = _dot_nt(qh, k) + b_ref[hh]
        m = jnp.max(s, axis=-1, keepdims=True)
        p = jnp.exp2(s - m)
        l = jnp.sum(p, axis=-1, keepdims=True)
        o = o + _dot(p.astype(BF16), vh) * (1.0 / l)
    o_ref[...] = o.astype(o_ref.dtype)


def _na_attention(q, k, v, rpb, seq):
    t = q.shape[0]
    nb = t // NA_TQ
    nbs = seq // NA_TQ
    bias = _na_bias(rpb)

    def prev(p, i):
        return (jnp.where(i % nbs == 0, i, i - 1), p)

    def nxt(p, i):
        return (jnp.where(i % nbs == nbs - 1, i, i + 1), p)

    def case(p, i):
        j = i % nbs
        return (p, jnp.where(j == 0, 0, jnp.where(j == nbs - 1, 2, 1)), 0, 0)

    cur = lambda p, i: (i, p)
    blk = lambda f: pl.BlockSpec((NA_TQ, 128), f)
    return pl.pallas_call(
        _na_attn_kernel,
        grid=(NA_PAIRS, nb),
        in_specs=[blk(cur), blk(prev), blk(cur), blk(nxt), blk(prev), blk(cur), blk(nxt),
                  pl.BlockSpec((2, None, NA_TQ, NA_TK), case)],
        out_specs=blk(cur),
        out_shape=jax.ShapeDtypeStruct((t, NA_HEADS * NA_HEAD_DIM), BF16),
        compiler_params=_params(("parallel", "arbitrary")),
    )(q, k, k, k, v, v, v, bias)


def _oproj_kernel(o_ref, w_ref, x_ref, gpost_ref, gpre_ref, perm_ref, x_out, h_out, edge_out):
    m = _dot(o_ref[...], w_ref[...])
    x1 = x_ref[...] + _rms(m) * gpost_ref[...]
    x_out[...] = x1
    h = _rms(x1) * gpre_ref[...]
    tm = h.shape[0]
    edge = jnp.concatenate([h[0:1], h[tm - 1:tm], jnp.zeros((HALO - 2, h.shape[1]), F32)], axis=0)
    edge_out[0] = edge.astype(BF16)
    h_out[...] = _dot(perm_ref[...], h.astype(BF16)).astype(BF16)


def _oproj(o, w_o, x, g_post, g_ffn_pre):
    t = x.shape[0]
    tm = TM_FFN
    row = lambda i: (i, 0)
    r = np.arange(tm)
    perm = np.zeros((tm, tm), np.float32)
    perm[r, (r % 8) * PERM_STRIDE + r // 8] = 1.0
    return pl.pallas_call(
        _oproj_kernel,
        grid=(t // tm,),
        in_specs=[pl.BlockSpec((tm, D_MODEL), row), _full((D_MODEL, D_MODEL)), pl.BlockSpec((tm, D_MODEL), row),
                  _full((1, D_MODEL)), _full((1, D_MODEL)), _full((tm, tm))],
        out_specs=[pl.BlockSpec((tm, D_MODEL), row), pl.BlockSpec((tm, D_MODEL), row),
                   pl.BlockSpec((1, HALO, D_MODEL), lambda i: (i, 0, 0))],
        out_shape=[jax.ShapeDtypeStruct((t, D_MODEL), F32), jax.ShapeDtypeStruct((t, D_MODEL), BF16),
                   jax.ShapeDtypeStruct((t // tm, HALO, D_MODEL), BF16)],
        compiler_params=_params(("parallel",)),
    )(o, w_o.astype(BF16), x, g_post.reshape(1, -1), g_ffn_pre.reshape(1, -1), jnp.asarray(perm, BF16))


def _ffn_kernel(h_ref, ep_ref, en_ref, wg_ref, wu_ref, cw_ref, cb_ref, wo_ref, x_ref, gpost_ref,
                p_ref, wproj_ref, wgate_ref, gple_ref, out_ref, hbuf, acc, *, tiles_per_seq):
    i = pl.program_id(0)
    f = pl.program_id(1)
    tm = h_ref.shape[0]

    @pl.when(f == 0)
    def _():
        first = (i % tiles_per_seq) == 0
        last = (i % tiles_per_seq) == tiles_per_seq - 1
        ep = ep_ref[0]
        en = en_ref[0]
        hbuf[0:tm, :] = h_ref[...]
        hbuf[tm:tm + HALO, :] = jnp.where(first, jnp.zeros_like(ep), ep)
        hbuf[tm + HALO:tm + 2 * HALO, :] = jnp.where(last, jnp.zeros_like(en), en)
        acc[...] = jnp.zeros_like(acc)

    g = _dot(hbuf[...], wg_ref[...])
    u = _dot(hbuf[0:tm, :], wu_ref[...])
    gm = g[0:tm]
    before = g[tm + 1:tm + 2]
    after = g[tm + HALO:tm + HALO + 1]
    sub = lax.broadcasted_iota(jnp.int32, (8, gm.shape[1]), 0)
    head = jnp.where(sub == 0, before, pltpu.roll(gm[tm - 8:tm], 1, 0))
    tail = jnp.where(sub == 7, after, pltpu.roll(gm[0:8], 7, 0))
    g_prev = jnp.concatenate([head, gm[0:tm - 8]], axis=0)
    g_next = jnp.concatenate([gm[8:tm], tail], axis=0)
    gc = g_prev * cw_ref[0:1, :] + gm * cw_ref[1:2, :] + g_next * cw_ref[2:3, :] + cb_ref[...]
    t = jnp.tanh(gc * (GELU_C1 + GELU_C2 * (gc * gc)))
    act = ((gc * u) * (0.5 + 0.5 * t)).astype(BF16)
    d = _dot(act, wo_ref[...])
    for c in range(acc.shape[0]):
        acc[c] += d[:, c * 128:(c + 1) * 128]

    @pl.when(f == pl.num_programs(1) - 1)
    def _():
        ffn = jnp.concatenate(
            [jnp.concatenate([acc[c, pl.ds(s, PERM_STRIDE, stride=8), :] for s in range(8)], axis=0)
             for c in range(acc.shape[0])], axis=1)
        x2 = x_ref[...] + _rms(ffn) * gpost_ref[...]
        e = _dot(p_ref[...].astype(BF16), wproj_ref[...])
        z = _dot(_rms(x2).astype(BF16), wgate_ref[...])
        gate = 1.0 / (1.0 + jnp.exp(-z))
        out_ref[...] = x2 + _rms(gate * e) * gple_ref[...]


def _ffn_ple(h, edge, x, p, layer, w_in, conv_w, conv_b, w_out, g_post, w_proj, w_gate, g_ple, seq):
    t = x.shape[0]
    tm, tf = TM_FFN, TF_FFN
    nf = D_FF // tf
    n_tiles = t // tm
    w_in = w_in.astype(BF16)
    row = lambda i, f: (i, 0)
    kern = functools.partial(_ffn_kernel, tiles_per_seq=seq // tm)
    return pl.pallas_call(
        kern,
        grid=(n_tiles, nf),
        in_specs=[pl.BlockSpec((tm, D_MODEL), row),
                  pl.BlockSpec((1, HALO, D_MODEL), lambda i, f: (jnp.maximum(i - 1, 0), 0, 0)),
                  pl.BlockSpec((1, HALO, D_MODEL), lambda i, f: (jnp.minimum(i + 1, n_tiles - 1), 0, 0)),
                  pl.BlockSpec((D_MODEL, tf), lambda i, f: (0, f)),
                  pl.BlockSpec((D_MODEL, tf), lambda i, f: (0, f + nf)),
                  pl.BlockSpec((3, tf), lambda i, f: (0, f)),
                  pl.BlockSpec((1, tf), lambda i, f: (0, f)),
                  pl.BlockSpec((tf, D_MODEL), lambda i, f: (f, 0)),
                  pl.BlockSpec((tm, D_MODEL), row),
                  _full((1, D_MODEL)),
                  pl.BlockSpec((None, tm, PLE_DIM), lambda i, f: (layer, i, 0)),
                  _full((PLE_DIM, D_MODEL)), _full((D_MODEL, D_MODEL)), _full((1, D_MODEL))],
        out_specs=pl.BlockSpec((tm, D_MODEL), row),
        out_shape=jax.ShapeDtypeStruct((t, D_MODEL), F32),
        scratch_shapes=[pltpu.VMEM((tm + 2 * HALO, D_MODEL), BF16), pltpu.VMEM((D_MODEL // 128, tm, 128), F32)],
        compiler_params=_params(("parallel", "arbitrary")),
    )(h, edge, edge, w_in, w_in, conv_w, conv_b.reshape(1, -1), w_out.astype(BF16), x, g_post.reshape(1, -1),
      p, w_proj.astype(BF16), w_gate.astype(BF16), g_ple.reshape(1, -1))


def _trunk(x, p, w, batch, seq):
    x = x.reshape(batch * seq, D_MODEL)
    p = p.reshape(DEPTH, batch * seq, PLE_DIM)
    for i in range(DEPTH):
        kind, j = i % N_MIXERS, i // N_MIXERS
        g_pre = w['norm_mix_pre'][i]
        if kind == 0:
            q, k, vt = _mla_pre(x, g_pre, w['mla_w_down'][j], w['mla_q_norm'][j], w['mla_kv_norm'][j],
                                w['mla_w_uq'][j], w['mla_w_ukv'][j], seq)
            o = _attention(q, k, vt, batch, seq, MLA_HEADS, 1, MLA_QK_PAD, MLA_V)
            w_o = w['mla_w_o'][j]
        elif kind == 1:
            q, k, vt = _gqa_pre(x, g_pre, w['gqa_w_qkv'][j], w['gqa_q_norm'][j], w['gqa_k_norm'][j], seq)
            o = _attention(q, k, vt, batch, seq, GQA_Q_HEADS, GQA_GROUP, GQA_HEAD_DIM, GQA_HEAD_DIM)
            w_o = w['gqa_w_o'][j]
        else:
            q, k, v = _na_pre(x, g_pre, w['na_w_qkv'][j])
            o = _na_attention(q, k, v, w['na_rpb'][j], seq)
            w_o = w['na_w_o'][j]
        x, h, edge = _oproj(o, w_o, x, w['norm_mix_post'][i], w['norm_ffn_pre'][i])
        x = _ffn_ple(h, edge, x, p, i, w['ffn_w_in'][i], w['ffn_conv_w'][i], w['ffn_conv_b'][i], w['ffn_w_out'][i],
                     w['norm_ffn_post'][i], w['ple_w_proj'][i], w['ple_w_gate'][i], w['ple_norm'][i], seq)
    return x.reshape(batch, seq, D_MODEL)


def kernel(x_prompt, x_sample, p_prompt, p_sample, norm_mix_pre, norm_mix_post, norm_ffn_pre, norm_ffn_post, mla_w_down, mla_q_norm, mla_kv_norm, mla_w_uq, mla_w_ukv, mla_w_o, gqa_w_qkv, gqa_q_norm, gqa_k_norm, gqa_w_o, na_w_qkv, na_rpb, na_w_o, ffn_w_in, ffn_conv_w, ffn_conv_b, ffn_w_out, ple_w_proj, ple_w_gate, ple_norm):
    w = dict(norm_mix_pre=norm_mix_pre, norm_mix_post=norm_mix_post, norm_ffn_pre=norm_ffn_pre,
             norm_ffn_post=norm_ffn_post, mla_w_down=mla_w_down, mla_q_norm=mla_q_norm,
             mla_kv_norm=mla_kv_norm, mla_w_uq=mla_w_uq, mla_w_ukv=mla_w_ukv, mla_w_o=mla_w_o,
             gqa_w_qkv=gqa_w_qkv, gqa_q_norm=gqa_q_norm, gqa_k_norm=gqa_k_norm, gqa_w_o=gqa_w_o,
             na_w_qkv=na_w_qkv, na_rpb=na_rpb, na_w_o=na_w_o, ffn_w_in=ffn_w_in,
             ffn_conv_w=ffn_conv_w, ffn_conv_b=ffn_conv_b, ffn_w_out=ffn_w_out,
             ple_w_proj=ple_w_proj, ple_w_gate=ple_w_gate, ple_norm=ple_norm)
    y_prompt = _trunk(x_prompt, p_prompt, w, *x_prompt.shape[:2])
    y_sample = _trunk(x_sample, p_sample, w, *x_sample.shape[:2])
    return (y_prompt, y_sample)
```

```python
import functools

import numpy as np
import jax
import jax.numpy as jnp
from jax import lax
from jax.experimental import pallas as pl
from jax.experimental.pallas import tpu as pltpu

F32 = jnp.float32
BF16 = jnp.bfloat16

D_MODEL = 1024
DEPTH = 4
N_MIXERS = 3
GRID_W = 64
ROPE_THETA = 10000.0
EPS = 1e-6

MLA_HEADS = 8
MLA_Q_LORA = 384
MLA_KV_LORA = 256
MLA_NOPE = 128
MLA_ROPE = 64
MLA_V = 128
MLA_QK_PAD = 256

GQA_Q_HEADS = 8
GQA_KV_HEADS = 2
GQA_HEAD_DIM = 128
GQA_GROUP = GQA_Q_HEADS // GQA_KV_HEADS

NA_HEADS = 16
NA_HEAD_DIM = 64
NA_WIN_ROWS = 8
NA_WIN_COLS = 16
NA_REL_ROWS = 2 * NA_WIN_ROWS - 1
NA_REL_COLS = 2 * NA_WIN_COLS - 1
NA_Q_ROWS = 4
NA_TQ = NA_Q_ROWS * GRID_W
NA_TK = 3 * NA_TQ
NA_PAIRS = NA_HEADS * NA_HEAD_DIM // 128
NA_PG = 4

D_FF = 4096
PLE_DIM = 256

NEG = -1e30
V7X_VMEM_LIMIT = 56 * 1024 * 1024
HALO = 16

TM_PROJ = 512
TM_FFN = 512
TF_FFN = 1024
PERM_STRIDE = TM_FFN // 8
GELU_C1 = 0.7978845608028654
GELU_C2 = GELU_C1 * 0.044715
TQ_ATTN = 512
TK_ATTN = 512
TKS_ATTN = 2048
LOG2E = 1.4426950408889634


def _params(sem):
    return pltpu.CompilerParams(dimension_semantics=sem, vmem_limit_bytes=V7X_VMEM_LIMIT)


def _rms(x):
    return x * lax.rsqrt(jnp.mean(x * x, axis=-1, keepdims=True) + EPS)


def _dot(a, b):
    return jnp.dot(a, b, preferred_element_type=F32)


def _dot_nt(a, b):
    return lax.dot_general(a, b, (((1,), (1,)), ((), ())), preferred_element_type=F32)


def _rope_tables(pos, dim):
    inv = (1.0 / (np.float32(ROPE_THETA) ** (np.arange(0, dim, 2, dtype=np.float32) / np.float32(dim)))).astype(np.float32)
    ang = np.asarray(pos, np.float32)[:, None] * inv[None, :]
    return np.cos(ang.astype(np.float64)).astype(np.float32), np.sin(ang.astype(np.float64)).astype(np.float32)


def _rot_cols(w):
    k, n = w.shape
    w4 = w.reshape(k, n // 64, 2, 32)
    return jnp.concatenate([-w4[:, :, 1:], w4[:, :, :1]], axis=2).reshape(k, n)


def _full(shape):
    return pl.BlockSpec(shape, lambda *_: (0,) * len(shape))


def _mla_pre_kernel(x_ref, g_ref, wd_ref, qn_ref, kvn_ref, wuq_ref, wuk_ref, wuvt_ref, qt_ref, kt_ref,
                    q_out, k_out, vt_out):
    hn = (_rms(x_ref[...]) * g_ref[...]).astype(BF16)
    down = _dot(hn, wd_ref[...])
    cq = (_rms(down[:, :MLA_Q_LORA]) * qn_ref[...]).astype(BF16)
    ckv = (_rms(down[:, MLA_Q_LORA:MLA_Q_LORA + MLA_KV_LORA]) * kvn_ref[...]).astype(BF16)
    kt = kt_ref[...]
    kr = (down[:, 640:768] * kt[:, :128] + down[:, 768:896] * kt[:, 128:]).astype(BF16)
    q = _dot(cq, wuq_ref[...])
    kn = _dot(ckv, wuk_ref[...])
    vt_out[0] = _dot_nt(wuvt_ref[...], ckv).astype(BF16)
    qt = qt_ref[...]
    for h in range(MLA_HEADS):
        lo = h * MLA_QK_PAD
        q_out[:, lo:lo + MLA_QK_PAD] = (q[:, lo:lo + MLA_QK_PAD] * qt).astype(BF16)
        k_out[:, lo:lo + MLA_NOPE] = kn[:, h * MLA_NOPE:(h + 1) * MLA_NOPE].astype(BF16)
        k_out[:, lo + MLA_NOPE:lo + MLA_QK_PAD] = kr


def _mla_pre(x, g, w_down, q_norm, kv_norm, w_uq, w_ukv, seq):
    t = x.shape[0]
    tm = TK_ATTN
    nseq = seq // tm
    wkr = w_down[:, MLA_Q_LORA + MLA_KV_LORA:]
    wkr_rot = _rot_cols(wkr)
    wd = jnp.concatenate([w_down[:, :MLA_Q_LORA + MLA_KV_LORA], wkr, wkr_rot, wkr_rot, wkr], axis=1).astype(BF16)
    wq3 = w_uq.reshape(MLA_Q_LORA, MLA_HEADS, MLA_NOPE + MLA_ROPE)
    wq_rope = wq3[:, :, MLA_NOPE:]
    wq_rot = _rot_cols(wq_rope.reshape(MLA_Q_LORA, -1)).reshape(MLA_Q_LORA, MLA_HEADS, MLA_ROPE)
    wuq = jnp.concatenate([wq3, wq_rot], axis=2).reshape(MLA_Q_LORA, MLA_HEADS * MLA_QK_PAD).astype(BF16)
    wkv3 = w_ukv.reshape(MLA_KV_LORA, MLA_HEADS, MLA_NOPE + MLA_V)
    wuk = wkv3[:, :, :MLA_NOPE].reshape(MLA_KV_LORA, -1).astype(BF16)
    wuvt = wkv3[:, :, MLA_NOPE:].reshape(MLA_KV_LORA, -1).T.astype(BF16)
    cos, sin = _rope_tables(np.arange(seq), MLA_ROPE)
    scale = np.float32((MLA_NOPE + MLA_ROPE) ** -0.5 * LOG2E)
    qt = jnp.asarray(np.concatenate([np.full((seq, MLA_NOPE), scale, np.float32), cos * scale, cos * scale,
                                     sin * scale, sin * scale], axis=1))
    kt = jnp.asarray(np.concatenate([cos, cos, sin, sin, sin, sin, cos, cos], axis=1))
    row = lambda i: (i, 0)
    pos = lambda i: (i % nseq, 0)
    return pl.pallas_call(
        _mla_pre_kernel,
        grid=(t // tm,),
        in_specs=[pl.BlockSpec((tm, D_MODEL), row), _full((1, D_MODEL)), _full(wd.shape),
                  _full((1, MLA_Q_LORA)), _full((1, MLA_KV_LORA)), _full(wuq.shape), _full(wuk.shape),
                  _full(wuvt.shape), pl.BlockSpec((tm, 256), pos), pl.BlockSpec((tm, 256), pos)],
        out_specs=[pl.BlockSpec((tm, MLA_HEADS * MLA_QK_PAD), row),
                   pl.BlockSpec((tm, MLA_HEADS * MLA_QK_PAD), row),
                   pl.BlockSpec((1, MLA_HEADS * MLA_V, tm), lambda i: (i, 0, 0))],
        out_shape=[jax.ShapeDtypeStruct((t, MLA_HEADS * MLA_QK_PAD), BF16),
                   jax.ShapeDtypeStruct((t, MLA_HEADS * MLA_QK_PAD), BF16),
                   jax.ShapeDtypeStruct((t // tm, MLA_HEADS * MLA_V, tm), BF16)],
        compiler_params=_params(("parallel",)),
    )(x, g.reshape(1, -1), wd, q_norm.reshape(1, -1), kv_norm.reshape(1, -1), wuq, wuk, wuvt, qt, kt)


def _gqa_pre_kernel(x_ref, g_ref, w_ref, wvt_ref, qg_ref, kg_ref, tab_ref, q_out, k_out, vt_out):
    hn = (_rms(x_ref[...]) * g_ref[...]).astype(BF16)
    y = _dot(hn, w_ref[...])
    vt_out[0] = _dot_nt(wvt_ref[...], hn).astype(BF16)
    tc = tab_ref[:, :128]
    ts = tab_ref[:, 128:]
    hd = GQA_HEAD_DIM
    nq = GQA_Q_HEADS * hd
    rot0 = nq + GQA_KV_HEADS * hd
    q_tc = qg_ref[0:1, :] * tc
    q_ts = qg_ref[1:2, :] * ts
    k_tc = kg_ref[0:1, :] * tc
    k_ts = kg_ref[1:2, :] * ts
    for h in range(GQA_Q_HEADS):
        z = y[:, h * hd:(h + 1) * hd]
        zr = y[:, rot0 + h * hd:rot0 + (h + 1) * hd]
        r = lax.rsqrt(jnp.mean(z * z, axis=-1, keepdims=True) + EPS) * (hd ** -0.5 * LOG2E)
        q_out[:, h * hd:(h + 1) * hd] = ((z * q_tc + zr * q_ts) * r).astype(BF16)
    for h in range(GQA_KV_HEADS):
        z = y[:, nq + h * hd:nq + (h + 1) * hd]
        zr = y[:, rot0 + nq + h * hd:rot0 + nq + (h + 1) * hd]
        r = lax.rsqrt(jnp.mean(z * z, axis=-1, keepdims=True) + EPS)
        k_out[:, h * hd:(h + 1) * hd] = ((z * k_tc + zr * k_ts) * r).astype(BF16)


def _gqa_pre(x, g, w_qkv, q_norm, k_norm, seq):
    t = x.shape[0]
    tm = TK_ATTN
    nseq = seq // tm
    hd = GQA_HEAD_DIM
    nq = GQA_Q_HEADS * hd
    nkv = GQA_KV_HEADS * hd
    wqk = w_qkv[:, :nq + nkv]
    w = jnp.concatenate([wqk, _rot_cols(wqk)], axis=1).astype(BF16)
    wvt = w_qkv[:, nq + nkv:].T.astype(BF16)
    pos = np.arange(seq)
    rc, rs = _rope_tables(pos // GRID_W, hd // 2)
    cc, cs = _rope_tables(pos % GRID_W, hd // 2)
    tab = jnp.asarray(np.concatenate([rc, rc, cc, cc, rs, rs, cs, cs], axis=1))

    def gains(gn):
        g4 = gn.reshape(2, 2, 32)
        return jnp.stack([gn, g4[:, ::-1].reshape(hd)])

    row = lambda i: (i, 0)
    return pl.pallas_call(
        _gqa_pre_kernel,
        grid=(t // tm,),
        in_specs=[pl.BlockSpec((tm, D_MODEL), row), _full((1, D_MODEL)), _full(w.shape), _full(wvt.shape),
                  _full((2, hd)), _full((2, hd)), pl.BlockSpec((tm, 256), lambda i: (i % nseq, 0))],
        out_specs=[pl.BlockSpec((tm, nq), row), pl.BlockSpec((tm, nkv), row),
                   pl.BlockSpec((1, nkv, tm), lambda i: (i, 0, 0))],
        out_shape=[jax.ShapeDtypeStruct((t, nq), BF16), jax.ShapeDtypeStruct((t, nkv), BF16),
                   jax.ShapeDtypeStruct((t // tm, nkv, tm), BF16)],
        compiler_params=_params(("parallel",)),
    )(x, g.reshape(1, -1), w, wvt, gains(q_norm), gains(k_norm), tab)


def _na_pre_kernel(x_ref, g_ref, w_ref, wvt_ref, q_out, k_out, vt_out):
    hn = (_rms(x_ref[...]) * g_ref[...]).astype(BF16)
    y = _dot(hn, w_ref[...])
    n = NA_HEADS * NA_HEAD_DIM
    q_out[...] = (y[:, :n] * (NA_HEAD_DIM ** -0.5 * LOG2E)).astype(BF16)
    k_out[...] = y[:, n:].astype(BF16)
    vt = _dot_nt(wvt_ref[...], hn).astype(BF16)
    for j in range(vt_out.shape[0]):
        vt_out[j] = vt[:, j * NA_TQ:(j + 1) * NA_TQ]


def _na_pre(x, g, w_qkv):
    t = x.shape[0]
    tm = TM_PROJ
    n = NA_HEADS * NA_HEAD_DIM
    per = tm // NA_TQ
    row = lambda i: (i, 0)
    return pl.pallas_call(
        _na_pre_kernel,
        grid=(t // tm,),
        in_specs=[pl.BlockSpec((tm, D_MODEL), row), _full((1, D_MODEL)), _full((D_MODEL, 2 * n)),
                  _full((n, D_MODEL))],
        out_specs=[pl.BlockSpec((tm, n), row), pl.BlockSpec((tm, n), row),
                   pl.BlockSpec((per, n, NA_TQ), lambda i: (i, 0, 0))],
        out_shape=[jax.ShapeDtypeStruct((t, n), BF16), jax.ShapeDtypeStruct((t, n), BF16),
                   jax.ShapeDtypeStruct((t // NA_TQ, n, NA_TQ), BF16)],
        compiler_params=_params(("parallel",)),
    )(x, g.reshape(1, -1), w_qkv[:, :2 * n].astype(BF16), w_qkv[:, 2 * n:].T.astype(BF16))


def _attn_kernel(q_ref, k_ref, vt_ref, o_ref, s_buf, acc_ref, *, tks):
    q = q_ref[...]
    tq = q.shape[0]
    tkv = vt_ref.shape[2]
    per = tks // tkv
    nk = k_ref.shape[0] // tks

    def scores(c, slot):
        off = pl.multiple_of(c * tks, tks)
        st = _dot_nt(k_ref[pl.ds(off, tks), :], q)
        s_buf[slot] = st
        return jnp.max(st, axis=0, keepdims=True)

    def accumulate(c, slot, cmax, m, l):
        m_new = jnp.maximum(m, cmax)
        alpha = jnp.exp2(m - m_new)
        pt = jnp.exp2(s_buf[slot] - m_new)
        l = alpha * l + jnp.sum(pt, axis=0, keepdims=True)
        pb = pt.astype(BF16)
        pv = _dot(vt_ref[c * per], pb[0:tkv])
        for j in range(1, per):
            pv = pv + _dot(vt_ref[c * per + j], pb[j * tkv:(j + 1) * tkv])
        acc_ref[...] = alpha * acc_ref[...] + pv
        return m_new, l

    def body(i, carry):
        m, l, cmax0 = carry
        c = 2 * i
        cmax1 = scores(c + 1, 1)
        m, l = accumulate(c, 0, cmax0, m, l)
        cmax0 = scores(c + 2, 0)
        m, l = accumulate(c + 1, 1, cmax1, m, l)
        return m, l, cmax0

    acc_ref[...] = jnp.zeros_like(acc_ref)
    init = (jnp.full((1, tq), NEG, F32), jnp.zeros((1, tq), F32), scores(0, 0))
    m, l, cmax0 = lax.fori_loop(0, nk // 2 - 1, body, init)
    cmax1 = scores(nk - 1, 1)
    m, l = accumulate(nk - 2, 0, cmax0, m, l)
    m, l = accumulate(nk - 1, 1, cmax1, m, l)
    o_ref[...] = (acc_ref[...] * (1.0 / l)).T.astype(o_ref.dtype)


def _attention(q, k, vt, batch, seq, heads, group, dk, dv):
    t = q.shape[0]
    tq, tk, tks = TQ_ATTN, TK_ATTN, min(TKS_ATTN, seq // 2)
    nq = seq // tq
    nk = seq // tk
    assert (seq // tks) % 2 == 0
    kern = functools.partial(_attn_kernel, tks=tks)
    return pl.pallas_call(
        kern,
        grid=(batch, heads, nq),
        in_specs=[pl.BlockSpec((tq, dk), lambda b, h, i: (b * nq + i, h)),
                  pl.BlockSpec((seq, dk), lambda b, h, i: (b, h // group)),
                  pl.BlockSpec((nk, dv, tk), lambda b, h, i: (b, h // group, 0))],
        out_specs=pl.BlockSpec((tq, dv), lambda b, h, i: (b * nq + i, h)),
        out_shape=jax.ShapeDtypeStruct((t, heads * dv), BF16),
        scratch_shapes=[pltpu.VMEM((2, tks, tq), F32), pltpu.VMEM((dv, tq), F32)],
        compiler_params=_params(("parallel", "parallel", "arbitrary")),
    )(q, k, vt)


def _na_bias(rpb):
    kc = np.arange(GRID_W)[:, None]
    qc = np.arange(GRID_W)[None, :]
    c0 = np.clip(qc - NA_WIN_COLS // 2, 0, GRID_W - NA_WIN_COLS)
    col_ok = (kc >= c0) & (kc < c0 + NA_WIN_COLS)
    rel_col = kc - qc + NA_WIN_COLS - 1
    onehot = ((rel_col[..., None] == np.arange(NA_REL_COLS)) & col_ok[..., None]).astype(np.float32)
    tile = jnp.einsum('hab,kqb->hakq', rpb * LOG2E, jnp.asarray(onehot), precision=lax.Precision.HIGHEST)
    tile = jnp.where(jnp.asarray(col_ok), tile, NEG)
    masked = jnp.full((NA_HEADS, GRID_W, GRID_W), NEG, F32)
    rows = 8 * NA_Q_ROWS
    cases = []
    for j in (0, 1, rows // NA_Q_ROWS - 1):
        blocks = []
        for qi in range(NA_Q_ROWS):
            r = NA_Q_ROWS * j + qi
            r0 = min(max(r - NA_WIN_ROWS // 2, 0), rows - NA_WIN_ROWS)
            tiles = []
            for kj in range(3 * NA_Q_ROWS):
                kr = NA_Q_ROWS * (j - 1) + kj
                ok = r0 <= kr < r0 + NA_WIN_ROWS
                tiles.append(tile[:, kr - r + NA_WIN_ROWS - 1] if ok else masked)
            blocks.append(jnp.concatenate(tiles, axis=1))
        cases.append(jnp.concatenate(blocks, axis=2))
    return jnp.stack(cases, axis=1)


def _na_attn_kernel(q_ref, kp_ref, kc_ref, kn_ref, vp_ref, vc_ref, vn_ref, b_ref, o_ref):
    lane = lax.broadcasted_iota(jnp.int32, (1, 128), 1)
    half = NA_HEAD_DIM
    heads = [(p, hh) for p in range(NA_PG) for hh in range(2)]

    def scores(p, hh):
        cols = slice(p * 128, (p + 1) * 128)
        q = q_ref[:, cols]
        k = jnp.concatenate([kp_ref[:, cols], kc_ref[:, cols], kn_ref[:, cols]], axis=0)
        sel = (lane < half) if hh == 0 else (lane >= half)
        return _dot_nt(k, jnp.where(sel, q, jnp.zeros_like(q))) + b_ref[2 * p + hh]

    def attend(p, hh, st):
        cols = slice(p * 128, (p + 1) * 128)
        vt = jnp.concatenate([vp_ref[0, cols, :], vc_ref[0, cols, :], vn_ref[0, cols, :]], axis=1)
        m = jnp.max(st, axis=0, keepdims=True)
        pt = jnp.exp2(st - m)
        l = jnp.sum(pt, axis=0, keepdims=True)
        ot = _dot(vt, pt.astype(BF16)) * (1.0 / l)
        return ot[hh * half:(hh + 1) * half]

    outs = []
    st = scores(*heads[0])
    for i, (p, hh) in enumerate(heads):
        st_next = scores(*heads[i + 1]) if i + 1 < len(heads) else None
        outs.append(attend(p, hh, st))
        st = st_next
    for p in range(NA_PG):
        o_ref[:, p * 128:(p + 1) * 128] = jnp.concatenate(outs[2 * p:2 * p + 2], axis=0).T.astype(o_ref.dtype)


def _na_attention(q, k, vt, rpb, seq):
    t = q.shape[0]
    nb = t // NA_TQ
    nbs = seq // NA_TQ
    bias = _na_bias(rpb)
    w = 128 * NA_PG

    def prev(p, i):
        return jnp.where(i % nbs == 0, i, i - 1)

    def nxt(p, i):
        return jnp.where(i % nbs == nbs - 1, i, i + 1)

    def case(p, i):
        j = i % nbs
        return (p, jnp.where(j == 0, 0, jnp.where(j == nbs - 1, 2, 1)), 0, 0)

    cur = lambda p, i: i
    blk = lambda f: pl.BlockSpec((NA_TQ, w), lambda p, i: (f(p, i), p))
    vblk = lambda f: pl.BlockSpec((1, w, NA_TQ), lambda p, i: (f(p, i), p, 0))
    return pl.pallas_call(
        _na_attn_kernel,
        grid=(NA_PAIRS // NA_PG, nb),
        in_specs=[blk(cur), blk(prev), blk(cur), blk(nxt), vblk(prev), vblk(cur), vblk(nxt),
                  pl.BlockSpec((2 * NA_PG, None, NA_TK, NA_TQ), case)],
        out_specs=blk(cur),
        out_shape=jax.ShapeDtypeStruct((t, NA_HEADS * NA_HEAD_DIM), BF16),
        compiler_params=_params(("parallel", "arbitrary")),
    )(q, k, k, k, vt, vt, vt, bias)


def _oproj_kernel(o_ref, w_ref, x_ref, gpost_ref, gpre_ref, perm_ref, x_out, h_out, edge_out):
    m = _dot(o_ref[...], w_ref[...])
    x1 = x_ref[...] + _rms(m) * gpost_ref[...]
    x_out[...] = x1
    h = _rms(x1) * gpre_ref[...]
    tm = h.shape[0]
    edge = jnp.concatenate([h[0:1], h[tm - 1:tm], jnp.zeros((HALO - 2, h.shape[1]), F32)], axis=0)
    edge_out[0] = edge.astype(BF16)
    h_out[...] = _dot(perm_ref[...], h.astype(BF16)).astype(BF16)


def _oproj(o, w_o, x, g_post, g_ffn_pre):
    t = x.shape[0]
    tm = TM_FFN
    row = lambda i: (i, 0)
    r = np.arange(tm)
    perm = np.zeros((tm, tm), np.float32)
    perm[r, (r % 8) * PERM_STRIDE + r // 8] = 1.0
    return pl.pallas_call(
        _oproj_kernel,
        grid=(t // tm,),
        in_specs=[pl.BlockSpec((tm, D_MODEL), row), _full((D_MODEL, D_MODEL)), pl.BlockSpec((tm, D_MODEL), row),
                  _full((1, D_MODEL)), _full((1, D_MODEL)), _full((tm, tm))],
        out_specs=[pl.BlockSpec((tm, D_MODEL), row), pl.BlockSpec((tm, D_MODEL), row),
                   pl.BlockSpec((1, HALO, D_MODEL), lambda i: (i, 0, 0))],
        out_shape=[jax.ShapeDtypeStruct((t, D_MODEL), F32), jax.ShapeDtypeStruct((t, D_MODEL), BF16),
                   jax.ShapeDtypeStruct((t // tm, HALO, D_MODEL), BF16)],
        compiler_params=_params(("parallel",)),
    )(o, w_o.astype(BF16), x, g_post.reshape(1, -1), g_ffn_pre.reshape(1, -1), jnp.asarray(perm, BF16))


def _ffn_kernel(h_ref, ep_ref, en_ref, wg_ref, wu_ref, cw_ref, cb_ref, wo_ref, x_ref, gpost_ref,
                p_ref, wproj_ref, wgate_ref, gple_ref, out_ref, hbuf, acc, *, tiles_per_seq):
    i = pl.program_id(0)
    f = pl.program_id(1)
    tm = h_ref.shape[0]

    @pl.when(f == 0)
    def _():
        first = (i % tiles_per_seq) == 0
        last = (i % tiles_per_seq) == tiles_per_seq - 1
        ep = ep_ref[0]
        en = en_ref[0]
        hbuf[0:tm, :] = h_ref[...]
        hbuf[tm:tm + HALO, :] = jnp.where(first, jnp.zeros_like(ep), ep)
        hbuf[tm + HALO:tm + 2 * HALO, :] = jnp.where(last, jnp.zeros_like(en), en)
        acc[...] = jnp.zeros_like(acc)

    g = _dot(hbuf[...], wg_ref[...])
    u = _dot(hbuf[0:tm, :], wu_ref[...])
    gm = g[0:tm]
    before = g[tm + 1:tm + 2]
    after = g[tm + HALO:tm + HALO + 1]
    sub = lax.broadcasted_iota(jnp.int32, (8, gm.shape[1]), 0)
    head = jnp.where(sub == 0, before, pltpu.roll(gm[tm - 8:tm], 1, 0))
    tail = jnp.where(sub == 7, after, pltpu.roll(gm[0:8], 7, 0))
    g_prev = jnp.concatenate([head, gm[0:tm - 8]], axis=0)
    g_next = jnp.concatenate([gm[8:tm], tail], axis=0)
    gc = g_prev * cw_ref[0:1, :] + gm * cw_ref[1:2, :] + g_next * cw_ref[2:3, :] + cb_ref[...]
    t = jnp.tanh(gc * (GELU_C1 + GELU_C2 * (gc * gc)))
    act = ((gc * u) * (0.5 + 0.5 * t)).astype(BF16)
    d = _dot(act, wo_ref[...])
    for c in range(acc.shape[0]):
        acc[c] += d[:, c * 128:(c + 1) * 128]

    @pl.when(f == pl.num_programs(1) - 1)
    def _():
        ffn = jnp.concatenate(
            [jnp.concatenate([acc[c, pl.ds(s, PERM_STRIDE, stride=8), :] for s in range(8)], axis=0)
             for c in range(acc.shape[0])], axis=1)
        x2 = x_ref[...] + _rms(ffn) * gpost_ref[...]
        e = _dot(p_ref[...].astype(BF16), wproj_ref[...])
        z = _dot(_rms(x2).astype(BF16), wgate_ref[...])
        gate = 1.0 / (1.0 + jnp.exp(-z))
        out_ref[...] = x2 + _rms(gate * e) * gple_ref[...]


def _ffn_ple(h, edge, x, p, layer, w_in, conv_w, conv_b, w_out, g_post, w_proj, w_gate, g_ple, seq):
    t = x.shape[0]
    tm, tf = TM_FFN, TF_FFN
    nf = D_FF // tf
    n_tiles = t // tm
    w_in = w_in.astype(BF16)
    row = lambda i, f: (i, 0)
    kern = functools.partial(_ffn_kernel, tiles_per_seq=seq // tm)
    return pl.pallas_call(
        kern,
        grid=(n_tiles, nf),
        in_specs=[pl.BlockSpec((tm, D_MODEL), row),
                  pl.BlockSpec((1, HALO, D_MODEL), lambda i, f: (jnp.maximum(i - 1, 0), 0, 0)),
                  pl.BlockSpec((1, HALO, D_MODEL), lambda i, f: (jnp.minimum(i + 1, n_tiles - 1), 0, 0)),
                  pl.BlockSpec((D_MODEL, tf), lambda i, f: (0, f)),
                  pl.BlockSpec((D_MODEL, tf), lambda i, f: (0, f + nf)),
                  pl.BlockSpec((3, tf), lambda i, f: (0, f)),
                  pl.BlockSpec((1, tf), lambda i, f: (0, f)),
                  pl.BlockSpec((tf, D_MODEL), lambda i, f: (f, 0)),
                  pl.BlockSpec((tm, D_MODEL), row),
                  _full((1, D_MODEL)),
                  pl.BlockSpec((None, tm, PLE_DIM), lambda i, f: (layer, i, 0)),
                  _full((PLE_DIM, D_MODEL)), _full((D_MODEL, D_MODEL)), _full((1, D_MODEL))],
        out_specs=pl.BlockSpec((tm, D_MODEL), row),
        out_shape=jax.ShapeDtypeStruct((t, D_MODEL), F32),
        scratch_shapes=[pltpu.VMEM((tm + 2 * HALO, D_MODEL), BF16), pltpu.VMEM((D_MODEL // 128, tm, 128), F32)],
        compiler_params=_params(("parallel", "arbitrary")),
    )(h, edge, edge, w_in, w_in, conv_w, conv_b.reshape(1, -1), w_out.astype(BF16), x, g_post.reshape(1, -1),
      p, w_proj.astype(BF16), w_gate.astype(BF16), g_ple.reshape(1, -1))


def _trunk(x, p, w, batch, seq):
    x = x.reshape(batch * seq, D_MODEL)
    p = p.reshape(DEPTH, batch * seq, PLE_DIM)
    for i in range(DEPTH):
        kind, j = i % N_MIXERS, i // N_MIXERS
        g_pre = w['norm_mix_pre'][i]
        if kind == 0:
            q, k, vt = _mla_pre(x, g_pre, w['mla_w_down'][j], w['mla_q_norm'][j], w['mla_kv_norm'][j],
                                w['mla_w_uq'][j], w['mla_w_ukv'][j], seq)
            o = _attention(q, k, vt, batch, seq, MLA_HEADS, 1, MLA_QK_PAD, MLA_V)
            w_o = w['mla_w_o'][j]
        elif kind == 1:
            q, k, vt = _gqa_pre(x, g_pre, w['gqa_w_qkv'][j], w['gqa_q_norm'][j], w['gqa_k_norm'][j], seq)
            o = _attention(q, k, vt, batch, seq, GQA_Q_HEADS, GQA_GROUP, GQA_HEAD_DIM, GQA_HEAD_DIM)
            w_o = w['gqa_w_o'][j]
        else:
            q, k, vt = _na_pre(x, g_pre, w['na_w_qkv'][j])
            o = _na_attention(q, k, vt, w['na_rpb'][j], seq)
            w_o = w['na_w_o'][j]
        x, h, edge = _oproj(o, w_o, x, w['norm_mix_post'][i], w['norm_ffn_pre'][i])
        x = _ffn_ple(h, edge, x, p, i, w['ffn_w_in'][i], w['ffn_conv_w'][i], w['ffn_conv_b'][i], w['ffn_w_out'][i],
                     w['norm_ffn_post'][i], w['ple_w_proj'][i], w['ple_w_gate'][i], w['ple_norm'][i], seq)
    return x.reshape(batch, seq, D_MODEL)


def kernel(x_prompt, x_sample, p_prompt, p_sample, norm_mix_pre, norm_mix_post, norm_ffn_pre, norm_ffn_post, mla_w_down, mla_q_norm, mla_kv_norm, mla_w_uq, mla_w_ukv, mla_w_o, gqa_w_qkv, gqa_q_norm, gqa_k_norm, gqa_w_o, na_w_qkv, na_rpb, na_w_o, ffn_w_in, ffn_conv_w, ffn_conv_b, ffn_w_out, ple_w_proj, ple_w_gate, ple_norm):
    w = dict(norm_mix_pre=norm_mix_pre, norm_mix_post=norm_mix_post, norm_ffn_pre=norm_ffn_pre,
             norm_ffn_post=norm_ffn_post, mla_w_down=mla_w_down, mla_q_norm=mla_q_norm,
             mla_kv_norm=mla_kv_norm, mla_w_uq=mla_w_uq, mla_w_ukv=mla_w_ukv, mla_w_o=mla_w_o,
             gqa_w_qkv=gqa_w_qkv, gqa_q_norm=gqa_q_norm, gqa_k_norm=gqa_k_norm, gqa_w_o=gqa_w_o,
             na_w_qkv=na_w_qkv, na_rpb=na_rpb, na_w_o=na_w_o, ffn_w_in=ffn_w_in,
             ffn_conv_w=ffn_conv_w, ffn_conv_b=ffn_conv_b, ffn_w_out=ffn_w_out,
             ple_w_proj=ple_w_proj, ple_w_gate=ple_w_gate, ple_norm=ple_norm)
    y_prompt = _trunk(x_prompt, p_prompt, w, *x_prompt.shape[:2])
    y_sample = _trunk(x_sample, p_sample, w, *x_sample.shape[:2])
    return (y_prompt, y_sample)
```

```python
import functools

import numpy as np
import jax
import jax.numpy as jnp
from jax import lax
from jax.experimental import pallas as pl
from jax.experimental.pallas import tpu as pltpu

F32 = jnp.float32
BF16 = jnp.bfloat16

D_MODEL = 1024
DEPTH = 4
N_MIXERS = 3
GRID_W = 64
ROPE_THETA = 10000.0
EPS = 1e-6

MLA_HEADS = 8
MLA_Q_LORA = 384
MLA_KV_LORA = 256
MLA_NOPE = 128
MLA_ROPE = 64
MLA_V = 128
MLA_QK_PAD = 256

GQA_Q_HEADS = 8
GQA_KV_HEADS = 2
GQA_HEAD_DIM = 128
GQA_GROUP = GQA_Q_HEADS // GQA_KV_HEADS

NA_HEADS = 16
NA_HEAD_DIM = 64
NA_WIN_ROWS = 8
NA_WIN_COLS = 16
NA_REL_ROWS = 2 * NA_WIN_ROWS - 1
NA_REL_COLS = 2 * NA_WIN_COLS - 1
NA_Q_ROWS = 4
NA_TQ = NA_Q_ROWS * GRID_W
NA_TK = 3 * NA_TQ
NA_PAIRS = NA_HEADS * NA_HEAD_DIM // 128
NA_PG = 4

D_FF = 4096
PLE_DIM = 256

NEG = -1e30
V7X_VMEM_LIMIT = 56 * 1024 * 1024
HALO = 16

TM_PROJ = 512
TM_FFN = 512
TF_FFN = 1024
PERM_STRIDE = TM_FFN // 8
GELU_C1 = 0.7978845608028654
GELU_C2 = GELU_C1 * 0.044715
TQ_ATTN = 512
TK_ATTN = 512
TKS_ATTN = 2048
LOG2E = 1.4426950408889634


def _params(sem):
    return pltpu.CompilerParams(dimension_semantics=sem, vmem_limit_bytes=V7X_VMEM_LIMIT)


def _rms(x):
    return x * lax.rsqrt(jnp.mean(x * x, axis=-1, keepdims=True) + EPS)


def _dot(a, b):
    return jnp.dot(a, b, preferred_element_type=F32)


def _dot_nt(a, b):
    return lax.dot_general(a, b, (((1,), (1,)), ((), ())), preferred_element_type=F32)


def _rope_tables(pos, dim):
    inv = (1.0 / (np.float32(ROPE_THETA) ** (np.arange(0, dim, 2, dtype=np.float32) / np.float32(dim)))).astype(np.float32)
    ang = np.asarray(pos, np.float32)[:, None] * inv[None, :]
    return np.cos(ang.astype(np.float64)).astype(np.float32), np.sin(ang.astype(np.float64)).astype(np.float32)


def _rot_cols(w):
    k, n = w.shape
    w4 = w.reshape(k, n // 64, 2, 32)
    return jnp.concatenate([-w4[:, :, 1:], w4[:, :, :1]], axis=2).reshape(k, n)


def _full(shape):
    return pl.BlockSpec(shape, lambda *_: (0,) * len(shape))


def _mla_pre_kernel(x_ref, g_ref, wd_ref, qn_ref, kvn_ref, wuq_ref, wuk_ref, wuvt_ref, qt_ref, kt_ref,
                    q_out, k_out, vt_out):
    hn = (_rms(x_ref[...]) * g_ref[...]).astype(BF16)
    down = _dot(hn, wd_ref[...])
    cq = (_rms(down[:, :MLA_Q_LORA]) * qn_ref[...]).astype(BF16)
    ckv = (_rms(down[:, MLA_Q_LORA:MLA_Q_LORA + MLA_KV_LORA]) * kvn_ref[...]).astype(BF16)
    kt = kt_ref[...]
    kr = (down[:, 640:768] * kt[:, :128] + down[:, 768:896] * kt[:, 128:]).astype(BF16)
    q = _dot(cq, wuq_ref[...])
    kn = _dot(ckv, wuk_ref[...])
    vt_out[0] = _dot_nt(wuvt_ref[...], ckv).astype(BF16)
    qt = qt_ref[...]
    for h in range(MLA_HEADS):
        lo = h * MLA_QK_PAD
        q_out[:, lo:lo + MLA_QK_PAD] = (q[:, lo:lo + MLA_QK_PAD] * qt).astype(BF16)
        k_out[:, lo:lo + MLA_NOPE] = kn[:, h * MLA_NOPE:(h + 1) * MLA_NOPE].astype(BF16)
        k_out[:, lo + MLA_NOPE:lo + MLA_QK_PAD] = kr


def _mla_pre(x, g, w_down, q_norm, kv_norm, w_uq, w_ukv, seq):
    t = x.shape[0]
    tm = TK_ATTN
    nseq = seq // tm
    wkr = w_down[:, MLA_Q_LORA + MLA_KV_LORA:]
    wkr_rot = _rot_cols(wkr)
    wd = jnp.concatenate([w_down[:, :MLA_Q_LORA + MLA_KV_LORA], wkr, wkr_rot, wkr_rot, wkr], axis=1).astype(BF16)
    wq3 = w_uq.reshape(MLA_Q_LORA, MLA_HEADS, MLA_NOPE + MLA_ROPE)
    wq_rope = wq3[:, :, MLA_NOPE:]
    wq_rot = _rot_cols(wq_rope.reshape(MLA_Q_LORA, -1)).reshape(MLA_Q_LORA, MLA_HEADS, MLA_ROPE)
    wuq = jnp.concatenate([wq3, wq_rot], axis=2).reshape(MLA_Q_LORA, MLA_HEADS * MLA_QK_PAD).astype(BF16)
    wkv3 = w_ukv.reshape(MLA_KV_LORA, MLA_HEADS, MLA_NOPE + MLA_V)
    wuk = wkv3[:, :, :MLA_NOPE].reshape(MLA_KV_LORA, -1).astype(BF16)
    wuvt = wkv3[:, :, MLA_NOPE:].reshape(MLA_KV_LORA, -1).T.astype(BF16)
    cos, sin = _rope_tables(np.arange(seq), MLA_ROPE)
    scale = np.float32((MLA_NOPE + MLA_ROPE) ** -0.5 * LOG2E)
    qt = jnp.asarray(np.concatenate([np.full((seq, MLA_NOPE), scale, np.float32), cos * scale, cos * scale,
                                     sin * scale, sin * scale], axis=1))
    kt = jnp.asarray(np.concatenate([cos, cos, sin, sin, sin, sin, cos, cos], axis=1))
    row = lambda i: (i, 0)
    pos = lambda i: (i % nseq, 0)
    return pl.pallas_call(
        _mla_pre_kernel,
        grid=(t // tm,),
        in_specs=[pl.BlockSpec((tm, D_MODEL), row), _full((1, D_MODEL)), _full(wd.shape),
                  _full((1, MLA_Q_LORA)), _full((1, MLA_KV_LORA)), _full(wuq.shape), _full(wuk.shape),
                  _full(wuvt.shape), pl.BlockSpec((tm, 256), pos), pl.BlockSpec((tm, 256), pos)],
        out_specs=[pl.BlockSpec((tm, MLA_HEADS * MLA_QK_PAD), row),
                   pl.BlockSpec((tm, MLA_HEADS * MLA_QK_PAD), row),
                   pl.BlockSpec((1, MLA_HEADS * MLA_V, tm), lambda i: (i, 0, 0))],
        out_shape=[jax.ShapeDtypeStruct((t, MLA_HEADS * MLA_QK_PAD), BF16),
                   jax.ShapeDtypeStruct((t, MLA_HEADS * MLA_QK_PAD), BF16),
                   jax.ShapeDtypeStruct((t // tm, MLA_HEADS * MLA_V, tm), BF16)],
        compiler_params=_params(("parallel",)),
    )(x, g.reshape(1, -1), wd, q_norm.reshape(1, -1), kv_norm.reshape(1, -1), wuq, wuk, wuvt, qt, kt)


def _gqa_pre_kernel(x_ref, g_ref, w_ref, wvt_ref, qg_ref, kg_ref, tab_ref, q_out, k_out, vt_out):
    hn = (_rms(x_ref[...]) * g_ref[...]).astype(BF16)
    y = _dot(hn, w_ref[...])
    vt_out[0] = _dot_nt(wvt_ref[...], hn).astype(BF16)
    tc = tab_ref[:, :128]
    ts = tab_ref[:, 128:]
    hd = GQA_HEAD_DIM
    nq = GQA_Q_HEADS * hd
    rot0 = nq + GQA_KV_HEADS * hd
    q_tc = qg_ref[0:1, :] * tc
    q_ts = qg_ref[1:2, :] * ts
    k_tc = kg_ref[0:1, :] * tc
    k_ts = kg_ref[1:2, :] * ts
    for h in range(GQA_Q_HEADS):
        z = y[:, h * hd:(h + 1) * hd]
        zr = y[:, rot0 + h * hd:rot0 + (h + 1) * hd]
        r = lax.rsqrt(jnp.mean(z * z, axis=-1, keepdims=True) + EPS) * (hd ** -0.5 * LOG2E)
        q_out[:, h * hd:(h + 1) * hd] = ((z * q_tc + zr * q_ts) * r).astype(BF16)
    for h in range(GQA_KV_HEADS):
        z = y[:, nq + h * hd:nq + (h + 1) * hd]
        zr = y[:, rot0 + nq + h * hd:rot0 + nq + (h + 1) * hd]
        r = lax.rsqrt(jnp.mean(z * z, axis=-1, keepdims=True) + EPS)
        k_out[:, h * hd:(h + 1) * hd] = ((z * k_tc + zr * k_ts) * r).astype(BF16)


def _gqa_pre(x, g, w_qkv, q_norm, k_norm, seq):
    t = x.shape[0]
    tm = TK_ATTN
    nseq = seq // tm
    hd = GQA_HEAD_DIM
    nq = GQA_Q_HEADS * hd
    nkv = GQA_KV_HEADS * hd
    wqk = w_qkv[:, :nq + nkv]
    w = jnp.concatenate([wqk, _rot_cols(wqk)], axis=1).astype(BF16)
    wvt = w_qkv[:, nq + nkv:].T.astype(BF16)
    pos = np.arange(seq)
    rc, rs = _rope_tables(pos // GRID_W, hd // 2)
    cc, cs = _rope_tables(pos % GRID_W, hd // 2)
    tab = jnp.asarray(np.concatenate([rc, rc, cc, cc, rs, rs, cs, cs], axis=1))

    def gains(gn):
        g4 = gn.reshape(2, 2, 32)
        return jnp.stack([gn, g4[:, ::-1].reshape(hd)])

    row = lambda i: (i, 0)
    return pl.pallas_call(
        _gqa_pre_kernel,
        grid=(t // tm,),
        in_specs=[pl.BlockSpec((tm, D_MODEL), row), _full((1, D_MODEL)), _full(w.shape), _full(wvt.shape),
                  _full((2, hd)), _full((2, hd)), pl.BlockSpec((tm, 256), lambda i: (i % nseq, 0))],
        out_specs=[pl.BlockSpec((tm, nq), row), pl.BlockSpec((tm, nkv), row),
                   pl.BlockSpec((1, nkv, tm), lambda i: (i, 0, 0))],
        out_shape=[jax.ShapeDtypeStruct((t, nq), BF16), jax.ShapeDtypeStruct((t, nkv), BF16),
                   jax.ShapeDtypeStruct((t // tm, nkv, tm), BF16)],
        compiler_params=_params(("parallel",)),
    )(x, g.reshape(1, -1), w, wvt, gains(q_norm), gains(k_norm), tab)


def _na_pre_kernel(x_ref, g_ref, w_ref, wvt_ref, q_out, k_out, vt_out):
    hn = (_rms(x_ref[...]) * g_ref[...]).astype(BF16)
    y = _dot(hn, w_ref[...])
    n = NA_HEADS * NA_HEAD_DIM
    q_out[...] = (y[:, :n] * (NA_HEAD_DIM ** -0.5 * LOG2E)).astype(BF16)
    k_out[...] = y[:, n:].astype(BF16)
    vt = _dot_nt(wvt_ref[...], hn).astype(BF16)
    for j in range(vt_out.shape[0]):
        vt_out[j] = vt[:, j * NA_TQ:(j + 1) * NA_TQ]


def _na_pre(x, g, w_qkv):
    t = x.shape[0]
    tm = TM_PROJ
    n = NA_HEADS * NA_HEAD_DIM
    per = tm // NA_TQ
    row = lambda i: (i, 0)
    return pl.pallas_call(
        _na_pre_kernel,
        grid=(t // tm,),
        in_specs=[pl.BlockSpec((tm, D_MODEL), row), _full((1, D_MODEL)), _full((D_MODEL, 2 * n)),
                  _full((n, D_MODEL))],
        out_specs=[pl.BlockSpec((tm, n), row), pl.BlockSpec((tm, n), row),
                   pl.BlockSpec((per, n, NA_TQ), lambda i: (i, 0, 0))],
        out_shape=[jax.ShapeDtypeStruct((t, n), BF16), jax.ShapeDtypeStruct((t, n), BF16),
                   jax.ShapeDtypeStruct((t // NA_TQ, n, NA_TQ), BF16)],
        compiler_params=_params(("parallel",)),
    )(x, g.reshape(1, -1), w_qkv[:, :2 * n].astype(BF16), w_qkv[:, 2 * n:].T.astype(BF16))


def _attn_kernel(q_ref, k_ref, vt_ref, o_ref, s_buf, acc_ref, *, tq, tks):
    tkv = vt_ref.shape[2]
    per = tks // tkv
    nq = q_ref.shape[0] // tq
    nk = k_ref.shape[0] // tks

    def scores(qi, c, slot):
        q = q_ref[pl.ds(pl.multiple_of(qi * tq, tq), tq), :]
        st = _dot_nt(k_ref[c * tks:(c + 1) * tks, :], q)
        s_buf[slot] = st
        return jnp.max(st, axis=0, keepdims=True)

    def accumulate(c, slot, cmax, m, l):
        m_new = jnp.maximum(m, cmax)
        alpha = jnp.exp2(m - m_new)
        pt = jnp.exp2(s_buf[slot] - m_new)
        l = alpha * l + jnp.sum(pt, axis=0, keepdims=True)
        pb = pt.astype(BF16)
        pv = _dot(vt_ref[c * per], pb[0:tkv])
        for j in range(1, per):
            pv = pv + _dot(vt_ref[c * per + j], pb[j * tkv:(j + 1) * tkv])
        acc_ref[...] = alpha * acc_ref[...] + pv
        return m_new, l

    def tile(qi, cmax, has_next):
        m = jnp.full((1, tq), NEG, F32)
        l = jnp.zeros((1, tq), F32)
        acc_ref[...] = jnp.zeros_like(acc_ref)
        for c in range(nk):
            slot = c % 2
            if c + 1 < nk:
                cmax_next = scores(qi, c + 1, 1 - slot)
            elif has_next:
                cmax_next = scores(qi + 1, 0, 1 - slot)
            else:
                cmax_next = None
            m, l = accumulate(c, slot, cmax, m, l)
            cmax = cmax_next
        o_ref[pl.ds(pl.multiple_of(qi * tq, tq), tq), :] = (acc_ref[...] * (1.0 / l)).T.astype(o_ref.dtype)
        return cmax

    cmax = scores(0, 0, 0)
    cmax = lax.fori_loop(0, nq - 1, lambda qi, cm: tile(qi, cm, True), cmax)
    tile(nq - 1, cmax, False)


def _attention(q, k, vt, batch, seq, heads, group, dk, dv):
    t = q.shape[0]
    tq, tk, tks = TQ_ATTN, TK_ATTN, min(TKS_ATTN, seq // 2)
    nk = seq // tk
    assert (seq // tks) % 2 == 0
    kern = functools.partial(_attn_kernel, tq=tq, tks=tks)
    return pl.pallas_call(
        kern,
        grid=(batch, heads),
        in_specs=[pl.BlockSpec((seq, dk), lambda b, h: (b, h)),
                  pl.BlockSpec((seq, dk), lambda b, h: (b, h // group)),
                  pl.BlockSpec((nk, dv, tk), lambda b, h: (b, h // group, 0))],
        out_specs=pl.BlockSpec((seq, dv), lambda b, h: (b, h)),
        out_shape=jax.ShapeDtypeStruct((t, heads * dv), BF16),
        scratch_shapes=[pltpu.VMEM((2, tks, tq), F32), pltpu.VMEM((dv, tq), F32)],
        compiler_params=_params(("parallel", "arbitrary")),
    )(q, k, vt)


def _na_bias(rpb):
    kc = np.arange(GRID_W)[:, None]
    qc = np.arange(GRID_W)[None, :]
    c0 = np.clip(qc - NA_WIN_COLS // 2, 0, GRID_W - NA_WIN_COLS)
    col_ok = (kc >= c0) & (kc < c0 + NA_WIN_COLS)
    rel_col = kc - qc + NA_WIN_COLS - 1
    onehot = ((rel_col[..., None] == np.arange(NA_REL_COLS)) & col_ok[..., None]).astype(np.float32)
    tile = jnp.einsum('hab,kqb->hakq', rpb * LOG2E, jnp.asarray(onehot), precision=lax.Precision.HIGHEST)
    tile = jnp.where(jnp.asarray(col_ok), tile, NEG)
    masked = jnp.full((NA_HEADS, GRID_W, GRID_W), NEG, F32)
    rows = 8 * NA_Q_ROWS
    cases = []
    for j in (0, 1, rows // NA_Q_ROWS - 1):
        blocks = []
        for qi in range(NA_Q_ROWS):
            r = NA_Q_ROWS * j + qi
            r0 = min(max(r - NA_WIN_ROWS // 2, 0), rows - NA_WIN_ROWS)
            tiles = []
            for kj in range(3 * NA_Q_ROWS):
                kr = NA_Q_ROWS * (j - 1) + kj
                ok = r0 <= kr < r0 + NA_WIN_ROWS
                tiles.append(tile[:, kr - r + NA_WIN_ROWS - 1] if ok else masked)
            blocks.append(jnp.concatenate(tiles, axis=1))
        cases.append(jnp.concatenate(blocks, axis=2))
    return jnp.stack(cases, axis=1)


def _na_attn_kernel(q_ref, kp_ref, kc_ref, kn_ref, vp_ref, vc_ref, vn_ref, b_ref, o_ref):
    lane = lax.broadcasted_iota(jnp.int32, (1, 128), 1)
    half = NA_HEAD_DIM
    heads = [(p, hh) for p in range(NA_PG) for hh in range(2)]

    def scores(p, hh):
        cols = slice(p * 128, (p + 1) * 128)
        q = q_ref[:, cols]
        k = jnp.concatenate([kp_ref[:, cols], kc_ref[:, cols], kn_ref[:, cols]], axis=0)
        sel = (lane < half) if hh == 0 else (lane >= half)
        return _dot_nt(k, jnp.where(sel, q, jnp.zeros_like(q))) + b_ref[2 * p + hh]

    def attend(p, hh, st):
        cols = slice(p * 128, (p + 1) * 128)
        vt = jnp.concatenate([vp_ref[0, cols, :], vc_ref[0, cols, :], vn_ref[0, cols, :]], axis=1)
        m = jnp.max(st, axis=0, keepdims=True)
        pt = jnp.exp2(st - m)
        l = jnp.sum(pt, axis=0, keepdims=True)
        ot = _dot(vt, pt.astype(BF16)) * (1.0 / l)
        return ot[hh * half:(hh + 1) * half]

    outs = []
    st = scores(*heads[0])
    for i, (p, hh) in enumerate(heads):
        st_next = scores(*heads[i + 1]) if i + 1 < len(heads) else None
        outs.append(attend(p, hh, st))
        st = st_next
    for p in range(NA_PG):
        o_ref[:, p * 128:(p + 1) * 128] = jnp.concatenate(outs[2 * p:2 * p + 2], axis=0).T.astype(o_ref.dtype)


def _na_attention(q, k, vt, rpb, seq):
    t = q.shape[0]
    nb = t // NA_TQ
    nbs = seq // NA_TQ
    bias = _na_bias(rpb)
    w = 128 * NA_PG

    def prev(p, i):
        return jnp.where(i % nbs == 0, i, i - 1)

    def nxt(p, i):
        return jnp.where(i % nbs == nbs - 1, i, i + 1)

    def case(p, i):
        j = i % nbs
        return (p, jnp.where(j == 0, 0, jnp.where(j == nbs - 1, 2, 1)), 0, 0)

    cur = lambda p, i: i
    blk = lambda f: pl.BlockSpec((NA_TQ, w), lambda p, i: (f(p, i), p))
    vblk = lambda f: pl.BlockSpec((1, w, NA_TQ), lambda p, i: (f(p, i), p, 0))
    return pl.pallas_call(
        _na_attn_kernel,
        grid=(NA_PAIRS // NA_PG, nb),
        in_specs=[blk(cur), blk(prev), blk(cur), blk(nxt), vblk(prev), vblk(cur), vblk(nxt),
                  pl.BlockSpec((2 * NA_PG, None, NA_TK, NA_TQ), case)],
        out_specs=blk(cur),
        out_shape=jax.ShapeDtypeStruct((t, NA_HEADS * NA_HEAD_DIM), BF16),
        compiler_params=_params(("parallel", "arbitrary")),
    )(q, k, k, k, vt, vt, vt, bias)


def _oproj_kernel(o_ref, w_ref, x_ref, gpost_ref, gpre_ref, perm_ref, x_out, h_out, edge_out):
    m = _dot(o_ref[...], w_ref[...])
    x1 = x_ref[...] + _rms(m) * gpost_ref[...]
    x_out[...] = x1
    h = _rms(x1) * gpre_ref[...]
    tm = h.shape[0]
    edge = jnp.concatenate([h[0:1], h[tm - 1:tm], jnp.zeros((HALO - 2, h.shape[1]), F32)], axis=0)
    edge_out[0] = edge.astype(BF16)
    h_out[...] = _dot(perm_ref[...], h.astype(BF16)).astype(BF16)


def _oproj(o, w_o, x, g_post, g_ffn_pre):
    t = x.shape[0]
    tm = TM_FFN
    row = lambda i: (i, 0)
    r = np.arange(tm)
    perm = np.zeros((tm, tm), np.float32)
    perm[r, (r % 8) * PERM_STRIDE + r // 8] = 1.0
    return pl.pallas_call(
        _oproj_kernel,
        grid=(t // tm,),
        in_specs=[pl.BlockSpec((tm, D_MODEL), row), _full((D_MODEL, D_MODEL)), pl.BlockSpec((tm, D_MODEL), row),
                  _full((1, D_MODEL)), _full((1, D_MODEL)), _full((tm, tm))],
        out_specs=[pl.BlockSpec((tm, D_MODEL), row), pl.BlockSpec((tm, D_MODEL), row),
                   pl.BlockSpec((1, HALO, D_MODEL), lambda i: (i, 0, 0))],
        out_shape=[jax.ShapeDtypeStruct((t, D_MODEL), F32), jax.ShapeDtypeStruct((t, D_MODEL), BF16),
                   jax.ShapeDtypeStruct((t // tm, HALO, D_MODEL), BF16)],
        compiler_params=_params(("parallel",)),
    )(o, w_o.astype(BF16), x, g_post.reshape(1, -1), g_ffn_pre.reshape(1, -1), jnp.asarray(perm, BF16))


def _ffn_kernel(h_ref, ep_ref, en_ref, wg_ref, wu_ref, cw_ref, cb_ref, wo_ref, x_ref, gpost_ref,
                p_ref, wproj_ref, wgate_ref, gple_ref, out_ref, hbuf, acc, *, tiles_per_seq):
    i = pl.program_id(0)
    f = pl.program_id(1)
    tm = h_ref.shape[0]

    @pl.when(f == 0)
    def _():
        first = (i % tiles_per_seq) == 0
        last = (i % tiles_per_seq) == tiles_per_seq - 1
        ep = ep_ref[0]
        en = en_ref[0]
        hbuf[0:tm, :] = h_ref[...]
        hbuf[tm:tm + HALO, :] = jnp.where(first, jnp.zeros_like(ep), ep)
        hbuf[tm + HALO:tm + 2 * HALO, :] = jnp.where(last, jnp.zeros_like(en), en)
        acc[...] = jnp.zeros_like(acc)

    g = _dot(hbuf[...], wg_ref[...])
    u = _dot(hbuf[0:tm, :], wu_ref[...])
    gm = g[0:tm]
    before = g[tm + 1:tm + 2]
    after = g[tm + HALO:tm + HALO + 1]
    sub = lax.broadcasted_iota(jnp.int32, (8, gm.shape[1]), 0)
    head = jnp.where(sub == 0, before, pltpu.roll(gm[tm - 8:tm], 1, 0))
    tail = jnp.where(sub == 7, after, pltpu.roll(gm[0:8], 7, 0))
    g_prev = jnp.concatenate([head, gm[0:tm - 8]], axis=0)
    g_next = jnp.concatenate([gm[8:tm], tail], axis=0)
    gc = g_prev * cw_ref[0:1, :] + gm * cw_ref[1:2, :] + g_next * cw_ref[2:3, :] + cb_ref[...]
    t = jnp.tanh(gc * (GELU_C1 + GELU_C2 * (gc * gc)))
    act = ((gc * u) * (0.5 + 0.5 * t)).astype(BF16)
    d = _dot(act, wo_ref[...])
    for c in range(acc.shape[0]):
        acc[c] += d[:, c * 128:(c + 1) * 128]

    @pl.when(f == pl.num_programs(1) - 1)
    def _():
        ffn = jnp.concatenate(
            [jnp.concatenate([acc[c, pl.ds(s, PERM_STRIDE, stride=8), :] for s in range(8)], axis=0)
             for c in range(acc.shape[0])], axis=1)
        x2 = x_ref[...] + _rms(ffn) * gpost_ref[...]
        e = _dot(p_ref[...].astype(BF16), wproj_ref[...])
        z = _dot(_rms(x2).astype(BF16), wgate_ref[...])
        gate = 1.0 / (1.0 + jnp.exp2(z * (-LOG2E)))
        out_ref[...] = x2 + _rms(gate * e) * gple_ref[...]


def _ffn_ple(h, edge, x, p, layer, w_in, conv_w, conv_b, w_out, g_post, w_proj, w_gate, g_ple, seq):
    t = x.shape[0]
    tm, tf = TM_FFN, TF_FFN
    nf = D_FF // tf
    n_tiles = t // tm
    w_in = w_in.astype(BF16)
    row = lambda i, f: (i, 0)
    kern = functools.partial(_ffn_kernel, tiles_per_seq=seq // tm)
    return pl.pallas_call(
        kern,
        grid=(n_tiles, nf),
        in_specs=[pl.BlockSpec((tm, D_MODEL), row),
                  pl.BlockSpec((1, HALO, D_MODEL), lambda i, f: (jnp.maximum(i - 1, 0), 0, 0)),
                  pl.BlockSpec((1, HALO, D_MODEL), lambda i, f: (jnp.minimum(i + 1, n_tiles - 1), 0, 0)),
                  pl.BlockSpec((D_MODEL, tf), lambda i, f: (0, f)),
                  pl.BlockSpec((D_MODEL, tf), lambda i, f: (0, f + nf)),
                  pl.BlockSpec((3, tf), lambda i, f: (0, f)),
                  pl.BlockSpec((1, tf), lambda i, f: (0, f)),
                  pl.BlockSpec((tf, D_MODEL), lambda i, f: (f, 0)),
                  pl.BlockSpec((tm, D_MODEL), row),
                  _full((1, D_MODEL)),
                  pl.BlockSpec((None, tm, PLE_DIM), lambda i, f: (layer, i, 0)),
                  _full((PLE_DIM, D_MODEL)), _full((D_MODEL, D_MODEL)), _full((1, D_MODEL))],
        out_specs=pl.BlockSpec((tm, D_MODEL), row),
        out_shape=jax.ShapeDtypeStruct((t, D_MODEL), F32),
        scratch_shapes=[pltpu.VMEM((tm + 2 * HALO, D_MODEL), BF16), pltpu.VMEM((D_MODEL // 128, tm, 128), F32)],
        compiler_params=_params(("parallel", "arbitrary")),
    )(h, edge, edge, w_in, w_in, conv_w, conv_b.reshape(1, -1), w_out.astype(BF16), x, g_post.reshape(1, -1),
      p, w_proj.astype(BF16), w_gate.astype(BF16), g_ple.reshape(1, -1))


def _trunk(x, p, w, batch, seq):
    x = x.reshape(batch * seq, D_MODEL)
    p = p.reshape(DEPTH, batch * seq, PLE_DIM)
    for i in range(DEPTH):
        kind, j = i % N_MIXERS, i // N_MIXERS
        g_pre = w['norm_mix_pre'][i]
        if kind == 0:
            q, k, vt = _mla_pre(x, g_pre, w['mla_w_down'][j], w['mla_q_norm'][j], w['mla_kv_norm'][j],
                                w['mla_w_uq'][j], w['mla_w_ukv'][j], seq)
            o = _attention(q, k, vt, batch, seq, MLA_HEADS, 1, MLA_QK_PAD, MLA_V)
            w_o = w['mla_w_o'][j]
        elif kind == 1:
            q, k, vt = _gqa_pre(x, g_pre, w['gqa_w_qkv'][j], w['gqa_q_norm'][j], w['gqa_k_norm'][j], seq)
            o = _attention(q, k, vt, batch, seq, GQA_Q_HEADS, GQA_GROUP, GQA_HEAD_DIM, GQA_HEAD_DIM)
            w_o = w['gqa_w_o'][j]
        else:
            q, k, vt = _na_pre(x, g_pre, w['na_w_qkv'][j])
            o = _na_attention(q, k, vt, w['na_rpb'][j], seq)
            w_o = w['na_w_o'][j]
        x, h, edge = _oproj(o, w_o, x, w['norm_mix_post'][i], w['norm_ffn_pre'][i])
        x = _ffn_ple(h, edge, x, p, i, w['ffn_w_in'][i], w['ffn_conv_w'][i], w['ffn_conv_b'][i], w['ffn_w_out'][i],
                     w['norm_ffn_post'][i], w['ple_w_proj'][i], w['ple_w_gate'][i], w['ple_norm'][i], seq)
    return x.reshape(batch, seq, D_MODEL)


def kernel(x_prompt, x_sample, p_prompt, p_sample, norm_mix_pre, norm_mix_post, norm_ffn_pre, norm_ffn_post, mla_w_down, mla_q_norm, mla_kv_norm, mla_w_uq, mla_w_ukv, mla_w_o, gqa_w_qkv, gqa_q_norm, gqa_k_norm, gqa_w_o, na_w_qkv, na_rpb, na_w_o, ffn_w_in, ffn_conv_w, ffn_conv_b, ffn_w_out, ple_w_proj, ple_w_gate, ple_norm):
    w = dict(norm_mix_pre=norm_mix_pre, norm_mix_post=norm_mix_post, norm_ffn_pre=norm_ffn_pre,
             norm_ffn_post=norm_ffn_post, mla_w_down=mla_w_down, mla_q_norm=mla_q_norm,
             mla_kv_norm=mla_kv_norm, mla_w_uq=mla_w_uq, mla_w_ukv=mla_w_ukv, mla_w_o=mla_w_o,
             gqa_w_qkv=gqa_w_qkv, gqa_q_norm=gqa_q_norm, gqa_k_norm=gqa_k_norm, gqa_w_o=gqa_w_o,
             na_w_qkv=na_w_qkv, na_rpb=na_rpb, na_w_o=na_w_o, ffn_w_in=ffn_w_in,
             ffn_conv_w=ffn_conv_w, ffn_conv_b=ffn_conv_b, ffn_w_out=ffn_w_out,
             ple_w_proj=ple_w_proj, ple_w_gate=ple_w_gate, ple_norm=ple_norm)
    y_prompt = _trunk(x_prompt, p_prompt, w, *x_prompt.shape[:2])
    y_sample = _trunk(x_sample, p_sample, w, *x_sample.shape[:2])
    return (y_prompt, y_sample)
```
